```python
import jax
import jax.numpy as jnp
from jax import lax
import numpy as np

D_MODEL = 1024
BATCH = 2
SEQ = 16384
DEPTH = 2

N_HEADS_MLA = 8
QK_NOPE_DIM = 64
QK_ROPE_DIM = 32
V_HEAD_DIM = 64
Q_LORA_RANK = 384
KV_LORA_RANK = 256
D_MLA = N_HEADS_MLA * V_HEAD_DIM

N_CONV_GROUPS = 4
CONV_GROUP_DIM = 64
D_CONV = N_CONV_GROUPS * CONV_GROUP_DIM
CONV_WIDTH = 3

N_FOURIER_GROUPS = 4
FOURIER_GROUP_DIM = 64
D_FOURIER = N_FOURIER_GROUPS * FOURIER_GROUP_DIM

D_MIX = D_MLA + D_CONV + D_FOURIER
IN_SPLITS = (Q_LORA_RANK, KV_LORA_RANK, QK_ROPE_DIM, D_CONV, D_CONV, D_CONV, D_FOURIER)
D_IN = sum(IN_SPLITS)

N_EXPERT_GROUPS = 4
EXPERTS_PER_GROUP = 8
TOP_K_IN_GROUP = 2
D_EXPERT = 256

Q_BLOCK = 128
ROPE_THETA = 10000.0
LN_EPS = 1e-5
RMS_EPS = 1e-6
DEEPNORM_ALPHA = (2.0 * DEPTH) ** 0.25
DEEPNORM_BETA = (8.0 * DEPTH) ** -0.25

kernel_name = 'hybrid_mla_conv_fourier_hmoe_deepnorm_encoder'


def layer_norm(x, g, b):
    xf = x.astype(jnp.float32)
    mu = jnp.mean(xf, -1, keepdims=True)
    xc = xf - mu
    var = jnp.mean(xc * xc, -1, keepdims=True)
    return (xc * lax.rsqrt(var + LN_EPS) * g.astype(jnp.float32) + b.astype(jnp.float32)).astype(x.dtype)


def rms_norm(x, g):
    xf = x.astype(jnp.float32)
    return (xf * lax.rsqrt(jnp.mean(xf * xf, -1, keepdims=True) + RMS_EPS) * g.astype(jnp.float32)).astype(x.dtype)


def rope_tables(seq_len, dtype):
    inv_freq = 1.0 / (ROPE_THETA ** (jnp.arange(0, QK_ROPE_DIM, 2, dtype=jnp.float32) / QK_ROPE_DIM))
    ang = jnp.arange(seq_len, dtype=jnp.float32)[:, None] * inv_freq[None, :]
    return jnp.cos(ang).astype(dtype), jnp.sin(ang).astype(dtype)


def apply_rope(t, cos, sin):
    t1, t2 = jnp.split(t, 2, axis=-1)
    return jnp.concatenate([t1 * cos - t2 * sin, t1 * sin + t2 * cos], axis=-1)


def mla_attention(c_q, c_kv, k_pe, g_q, g_kv, w_uq, w_ukv, cos, sin):
    bsz, seq, _ = c_q.shape
    q = jnp.einsum('bsr,rhd->bshd', rms_norm(c_q, g_q), w_uq)
    kv = jnp.einsum('bsr,rhd->bshd', rms_norm(c_kv, g_kv), w_ukv)
    q_nope = q[..., :QK_NOPE_DIM]
    q_pe = apply_rope(q[..., QK_NOPE_DIM:], cos[None, :, None, :], sin[None, :, None, :])
    k_nope = kv[..., :QK_NOPE_DIM]
    v = kv[..., QK_NOPE_DIM:]
    k_pe = apply_rope(k_pe, cos[None], sin[None])
    scale = (QK_NOPE_DIM + QK_ROPE_DIM) ** -0.5
    n_blk = seq // Q_BLOCK

    def to_blocks(t):
        return (t * scale).reshape(bsz, n_blk, Q_BLOCK, N_HEADS_MLA, t.shape[-1]).swapaxes(0, 1)

    def attend(blk):
        qn, qp = blk
        s = jnp.einsum('bqhd,bkhd->bhqk', qn, k_nope) + jnp.einsum('bqhr,bkr->bhqk', qp, k_pe)
        p = jax.nn.softmax(s.astype(jnp.float32), axis=-1).astype(v.dtype)
        return jnp.einsum('bhqk,bkhd->bqhd', p, v)

    o = lax.map(attend, (to_blocks(q_nope), to_blocks(q_pe)))
    return o.swapaxes(0, 1).reshape(bsz, seq, D_MLA)


def short_conv_mixer(b_gate, c_gate, h, w_conv):
    seq = h.shape[1]
    pad = CONV_WIDTH // 2
    up = jnp.pad(c_gate * h, ((0, 0), (pad, pad), (0, 0)))
    y = sum(up[:, k:k + seq] * w_conv[k] for k in range(CONV_WIDTH))
    return b_gate * y


def fourier_mixer(f):
    bsz, seq, _ = f.shape
    fg = f.astype(jnp.float32).reshape(bsz, seq, N_FOURIER_GROUPS, FOURIER_GROUP_DIM).transpose(0, 2, 1, 3)
    y = jnp.fft.fft2(fg, norm='ortho').real
    return y.transpose(0, 2, 1, 3).reshape(bsz, seq, D_FOURIER).astype(f.dtype)


def token_mixer(h, w_in, g_q, g_kv, w_uq, w_ukv, w_conv, w_out, cos, sin):
    z = jnp.einsum('bsd,dp->bsp', h, w_in)
    split_points = np.cumsum(IN_SPLITS)[:-1].tolist()
    c_q, c_kv, k_pe, b_gate, c_gate, hc, f = jnp.split(z, split_points, axis=-1)
    o_mla = mla_attention(c_q, c_kv, k_pe, g_q, g_kv, w_uq, w_ukv, cos, sin)
    o_conv = short_conv_mixer(b_gate, c_gate, hc, w_conv)
    o_four = fourier_mixer(f)
    o = jnp.concatenate([o_mla, o_conv, o_four], axis=-1)
    return jnp.einsum('bsm,md->bsd', o, w_out)


def hier_moe(h, w_group, b_group, w_router, b_router, w_gate, w_up, w_down):
    bsz, seq, d = h.shape
    xt = h.reshape(-1, d)
    g_prob = jax.nn.softmax((xt @ w_group + b_group).astype(jnp.float32), axis=-1)
    g_onehot = jax.nn.one_hot(jnp.argmax(g_prob, axis=-1), N_EXPERT_GROUPS, dtype=jnp.float32)
    p_group = jnp.sum(g_prob * g_onehot, -1, keepdims=True)
    e_logits_all = (jnp.einsum('nd,dge->nge', xt, w_router) + b_router).astype(jnp.float32)
    e_logits = jnp.einsum('nge,ng->ne', e_logits_all, g_onehot)
    e_prob = jax.nn.softmax(e_logits, axis=-1)
    top_p, top_i = lax.top_k(e_prob, TOP_K_IN_GROUP)
    top_p = top_p / jnp.sum(top_p, -1, keepdims=True)
    w_in_group = jnp.sum(jax.nn.one_hot(top_i, EXPERTS_PER_GROUP, dtype=jnp.float32) * top_p[..., None], axis=1) * p_group
    combine = (g_onehot[:, :, None] * w_in_group[:, None, :]).astype(h.dtype)
    out = jnp.zeros_like(xt)
    for g in range(N_EXPERT_GROUPS):
        a = jnp.einsum('nd,edf->nef', xt, w_gate[g])
        u = jnp.einsum('nd,edf->nef', xt, w_up[g])
        hid = jax.nn.silu(a) * u * combine[:, g, :, None]
        out = out + jnp.einsum('nef,efd->nd', hid, w_down[g])
    return out.reshape(bsz, seq, d)


def setup_inputs(seed: int = 0) -> dict:
    key = jax.random.key(seed)
    ks = jax.random.split(key, 24)
    f32 = jnp.float32
    L = DEPTH

    def nrm(k, shape, scale):
        return jax.random.normal(k, shape, f32) * scale

    return {
        'x': nrm(ks[0], (BATCH, SEQ, D_MODEL), 1.0),
        'ln_in_g': 1.0 + nrm(ks[1], (D_MODEL,), 0.02),
        'ln_in_b': nrm(ks[2], (D_MODEL,), 0.02),
        'w_in': nrm(ks[3], (L, D_MODEL, D_IN), D_MODEL ** -0.5),
        'g_q': 1.0 + nrm(ks[4], (L, Q_LORA_RANK), 0.02),
        'g_kv': 1.0 + nrm(ks[5], (L, KV_LORA_RANK), 0.02),
        'w_uq': nrm(ks[6], (L, Q_LORA_RANK, N_HEADS_MLA, QK_NOPE_DIM + QK_ROPE_DIM), Q_LORA_RANK ** -0.5),
        'w_ukv': nrm(ks[7], (L, KV_LORA_RANK, N_HEADS_MLA, QK_NOPE_DIM + V_HEAD_DIM), KV_LORA_RANK ** -0.5),
        'w_conv': nrm(ks[8], (L, CONV_WIDTH, D_CONV), CONV_WIDTH ** -0.5),
        'w_out': nrm(ks[9], (L, D_MIX, D_MODEL), DEEPNORM_BETA * D_MIX ** -0.5),
        'ln1_g': 1.0 + nrm(ks[10], (L, D_MODEL), 0.02),
        'ln1_b': nrm(ks[11], (L, D_MODEL), 0.02),
        'w_group': nrm(ks[12], (L, D_MODEL, N_EXPERT_GROUPS), D_MODEL ** -0.5),
        'b_group': nrm(ks[13], (L, N_EXPERT_GROUPS), 0.01),
        'w_router': nrm(ks[14], (L, D_MODEL, N_EXPERT_GROUPS, EXPERTS_PER_GROUP), D_MODEL ** -0.5),
        'b_router': nrm(ks[15], (L, N_EXPERT_GROUPS, EXPERTS_PER_GROUP), 0.01),
        'w_gate': nrm(ks[16], (L, N_EXPERT_GROUPS, EXPERTS_PER_GROUP, D_MODEL, D_EXPERT), D_MODEL ** -0.5),
        'w_up': nrm(ks[17], (L, N_EXPERT_GROUPS, EXPERTS_PER_GROUP, D_MODEL, D_EXPERT), D_MODEL ** -0.5),
        'w_down': nrm(ks[18], (L, N_EXPERT_GROUPS, EXPERTS_PER_GROUP, D_EXPERT, D_MODEL), DEEPNORM_BETA * D_EXPERT ** -0.5),
        'ln2_g': 1.0 + nrm(ks[19], (L, D_MODEL), 0.02),
        'ln2_b': nrm(ks[20], (L, D_MODEL), 0.02),
    }


def reference(x, ln_in_g, ln_in_b, w_in, g_q, g_kv, w_uq, w_ukv, w_conv, w_out, ln1_g, ln1_b,
              w_group, b_group, w_router, b_router, w_gate, w_up, w_down, ln2_g, ln2_b):
    cos, sin = rope_tables(x.shape[1], x.dtype)
    h = layer_norm(x, ln_in_g, ln_in_b)
    for l in range(DEPTH):
        mix = token_mixer(h, w_in[l], g_q[l], g_kv[l], w_uq[l], w_ukv[l], w_conv[l], w_out[l], cos, sin)
        h = layer_norm(DEEPNORM_ALPHA * h + mix, ln1_g[l], ln1_b[l])
        ffn = hier_moe(h, w_group[l], b_group[l], w_router[l], b_router[l], w_gate[l], w_up[l], w_down[l])
        h = layer_norm(DEEPNORM_ALPHA * h + ffn, ln2_g[l], ln2_b[l])
    return h
```

```python
import functools
import math

import numpy as np
import jax
import jax.numpy as jnp
from jax import lax
from jax.experimental import pallas as pl
from jax.experimental.pallas import tpu as pltpu

N_HEADS = 8
QK_NOPE = 64
QK_ROPE = 32
D_QK = QK_NOPE + QK_ROPE
V_DIM = 64
Q_LORA = 384
KV_LORA = 256
D_CONV = 256
D_FOURIER = 256
FOURIER_GROUP = 64
N_GROUPS = 4
EXPERTS_PER_GROUP = 8
N_EXPERTS = N_GROUPS * EXPERTS_PER_GROUP
D_EXPERT = 256
ROPE_THETA = 10000.0
LN_EPS = 1e-5
RMS_EPS = 1e-6
DFT_N1 = 128
ROUTER_LANES = 128
VMEM_LIMIT = 56 * 1024 * 1024

_O_CQ = 0
_O_CKV = _O_CQ + Q_LORA
_O_B = _O_CKV + KV_LORA
_O_C = _O_B + D_CONV
_O_H = _O_C + D_CONV
_O_F = _O_H + D_CONV
_O_KPE = _O_F + D_FOURIER
_O_KPES = _O_KPE + 128
_W_IN_COLS = _O_KPES + 128


def _cparams(sem):
    return pltpu.CompilerParams(dimension_semantics=sem, vmem_limit_bytes=VMEM_LIMIT)


def _layer_norm(x, g, b):
    mu = jnp.mean(x, axis=-1, keepdims=True)
    xc = x - mu
    var = jnp.mean(xc * xc, axis=-1, keepdims=True)
    return xc * lax.rsqrt(var + LN_EPS) * g + b


def _rms_norm(x, g):
    return x * lax.rsqrt(jnp.mean(x * x, axis=-1, keepdims=True) + RMS_EPS) * g


def _bdot(a, b):
    return jnp.dot(a, b, preferred_element_type=jnp.float32)


def _inproj_kernel(x_ref, lng_ref, lnb_ref, win_ref, gq_ref, gkv_ref, wq_ref, wqs_ref, wk_ref, wv_ref,
                   c96_ref, s96_ref, cc_ref, sc_ref, *out_refs, ln_input, q_scale):
    if ln_input:
        h_ref, q_ref, k_ref, v_ref, u_ref, b_ref, ab_ref = out_refs
        h = _layer_norm(x_ref[0], lng_ref[...], lnb_ref[...])
        h_ref[0] = h
    else:
        q_ref, k_ref, v_ref, u_ref, b_ref, ab_ref = out_refs
        h = x_ref[0]
    z = _bdot(h.astype(jnp.bfloat16), win_ref[...])
    c96 = c96_ref[...]
    s96 = s96_ref[...]
    cqn = _rms_norm(z[:, _O_CQ:_O_CQ + Q_LORA], gq_ref[...]).astype(jnp.bfloat16)
    ckvn = _rms_norm(z[:, _O_CKV:_O_CKV + KV_LORA], gkv_ref[...]).astype(jnp.bfloat16)
    kpe = z[:, _O_KPE:_O_KPE + D_QK] * c96 + z[:, _O_KPES:_O_KPES + D_QK] * s96
    for hd in range(N_HEADS):
        qh = (_bdot(cqn, wq_ref[hd]) * c96 + _bdot(cqn, wqs_ref[hd]) * s96) * q_scale
        q_ref[0, hd] = qh.astype(q_ref.dtype)
        k_ref[0, hd] = (_bdot(ckvn, wk_ref[hd]) + kpe).astype(k_ref.dtype)
        v_ref[0, hd] = _bdot(ckvn, wv_ref[hd]).astype(v_ref.dtype)
    u_ref[0] = z[:, _O_C:_O_C + D_CONV] * z[:, _O_H:_O_H + D_CONV]
    b_ref[0] = z[:, _O_B:_O_B + D_CONV]
    fb = z[:, _O_F:_O_F + D_FOURIER].astype(jnp.bfloat16)
    ab_ref[0, :, :D_FOURIER] = _bdot(fb, cc_ref[...]).astype(ab_ref.dtype)
    ab_ref[0, :, D_FOURIER:] = _bdot(fb, sc_ref[...]).astype(ab_ref.dtype)


def _inproj(x, lng, lnb, win, gq, gkv, wq, wqs, wk, wv, c96, s96, cc, sc, *, ln_input, tm):
    bsz, seq, d = x.shape
    q_scale = (D_QK ** -0.5) * math.log2(math.e)
    grid = (bsz, seq // tm)
    const2 = lambda b, i: (0, 0)
    const3 = lambda b, i: (0, 0, 0)
    in_specs = [
        pl.BlockSpec((1, tm, d), lambda b, i: (b, i, 0)),
        pl.BlockSpec((1, d), const2),
        pl.BlockSpec((1, d), const2),
        pl.BlockSpec(win.shape, const2),
        pl.BlockSpec((1, Q_LORA), const2),
        pl.BlockSpec((1, KV_LORA), const2),
        pl.BlockSpec(wq.shape, const3),
        pl.BlockSpec(wqs.shape, const3),
        pl.BlockSpec(wk.shape, const3),
        pl.BlockSpec(wv.shape, const3),
        pl.BlockSpec((tm, D_QK), lambda b, i: (i, 0)),
        pl.BlockSpec((tm, D_QK), lambda b, i: (i, 0)),
        pl.BlockSpec(cc.shape, const2),
        pl.BlockSpec(sc.shape, const2),
    ]
    head_spec = lambda w: pl.BlockSpec((1, N_HEADS, tm, w), lambda b, i: (b, 0, i, 0))
    tok_spec = lambda w: pl.BlockSpec((1, tm, w), lambda b, i: (b, i, 0))
    out_shape = [
        jax.ShapeDtypeStruct((bsz, N_HEADS, seq, D_QK), jnp.bfloat16),
        jax.ShapeDtypeStruct((bsz, N_HEADS, seq, D_QK), jnp.bfloat16),
        jax.ShapeDtypeStruct((bsz, N_HEADS, seq, V_DIM), jnp.bfloat16),
        jax.ShapeDtypeStruct((bsz, seq, D_CONV), jnp.float32),
        jax.ShapeDtypeStruct((bsz, seq, D_CONV), jnp.float32),
        jax.ShapeDtypeStruct((bsz, seq, 2 * D_FOURIER), jnp.bfloat16),
    ]
    out_specs = [head_spec(D_QK), head_spec(D_QK), head_spec(V_DIM), tok_spec(D_CONV), tok_spec(D_CONV),
                 tok_spec(2 * D_FOURIER)]
    if ln_input:
        out_shape = [jax.ShapeDtypeStruct((bsz, seq, d), jnp.float32)] + out_shape
        out_specs = [tok_spec(d)] + out_specs
    return pl.pallas_call(
        functools.partial(_inproj_kernel, ln_input=ln_input, q_scale=q_scale),
        grid=grid, in_specs=in_specs, out_specs=out_specs, out_shape=out_shape,
        compiler_params=_cparams(("parallel", "parallel")), name="inproj",
    )(x, lng, lnb, win, gq, gkv, wq, wqs, wk, wv, c96, s96, cc, sc)


def _attn_kernel(q_ref, k_ref, v_ref, o_ref, *, tk, n_kv):
    q = q_ref[0, 0]
    tq = q.shape[0]

    def body(j, carry):
        m, l, acc = carry
        off = pl.multiple_of(j * tk, tk)
        kc = k_ref[0, 0, pl.ds(off, tk), :]
        vc = v_ref[0, 0, pl.ds(off, tk), :]
        s = lax.dot_general(q, kc, (((1,), (1,)), ((), ())), preferred_element_type=jnp.float32)
        m_new = jnp.maximum(m, jnp.max(s, axis=-1, keepdims=True))
        alpha = jnp.exp2(m - m_new)
        p = jnp.exp2(s - m_new)
        l = alpha * l + jnp.sum(p, axis=-1, keepdims=True)
        acc = alpha * acc + _bdot(p.astype(jnp.bfloat16), vc)
        return m_new, l, acc

    init = (jnp.full((tq, 1), -jnp.inf, jnp.float32), jnp.zeros((tq, 1), jnp.float32),
            jnp.zeros((tq, V_DIM), jnp.float32))
    _, l, acc = lax.fori_loop(0, n_kv, body, init)
    o_ref[0, 0] = (acc / l).astype(o_ref.dtype)


def _attention(q, k, v, *, tq, tk):
    bsz, nh, seq, _ = q.shape
    return pl.pallas_call(
        functools.partial(_attn_kernel, tk=tk, n_kv=seq // tk),
        grid=(bsz, nh, seq // tq),
        in_specs=[
            pl.BlockSpec((1, 1, tq, D_QK), lambda b, h, i: (b, h, i, 0)),
            pl.BlockSpec((1, 1, seq, D_QK), lambda b, h, i: (b, h, 0, 0)),
            pl.BlockSpec((1, 1, seq, V_DIM), lambda b, h, i: (b, h, 0, 0)),
        ],
        out_specs=pl.BlockSpec((1, 1, tq, V_DIM), lambda b, h, i: (b, h, i, 0)),
        out_shape=jax.ShapeDtypeStruct((bsz, nh, seq, V_DIM), jnp.bfloat16),
        compiler_params=_cparams(("parallel", "parallel", "arbitrary")), name="attention",
    )(q, k, v)


def _fourier_a_kernel(ab_ref, m1a_ref, m1b_ref, y_ref, *, n2c):
    for j in range(n2c):
        a = ab_ref[0, :, j * 2 * D_FOURIER: j * 2 * D_FOURIER + D_FOURIER]
        b = ab_ref[0, :, j * 2 * D_FOURIER + D_FOURIER: (j + 1) * 2 * D_FOURIER]
        y = _bdot(m1a_ref[...], a) + _bdot(m1b_ref[...], b)
        y_ref[0, :, j * D_FOURIER:(j + 1) * D_FOURIER] = y.astype(y_ref.dtype)


def _fourier_b_kernel(yr_ref, yi_ref, m2_ref, o_ref, *, k1c, n2):
    for j in range(k1c):
        m2 = m2_ref[j]
        z = _bdot(m2[:, :n2], yr_ref[0, j]) + _bdot(m2[:, n2:], yi_ref[0, j])
        o_ref[0, :, j, :] = z.astype(o_ref.dtype)


def _fourier(ab, m1a, m1b, m2):
    bsz, seq, _ = ab.shape
    n1 = DFT_N1
    n2 = seq // n1
    n2c = min(n2, 16)
    k1c = 8
    abv = ab.reshape(bsz, n1, n2 * 2 * D_FOURIER)
    y = pl.pallas_call(
        functools.partial(_fourier_a_kernel, n2c=n2c),
        grid=(bsz, n2 // n2c),
        in_specs=[
            pl.BlockSpec((1, n1, n2c * 2 * D_FOURIER), lambda b, c: (b, 0, c)),
            pl.BlockSpec(m1a.shape, lambda b, c: (0, 0)),
            pl.BlockSpec(m1b.shape, lambda b, c: (0, 0)),
        ],
        out_specs=pl.BlockSpec((1, 2 * n1, n2c * D_FOURIER), lambda b, c: (b, 0, c)),
        out_shape=jax.ShapeDtypeStruct((bsz, 2 * n1, n2 * D_FOURIER), jnp.bfloat16),
        compiler_params=_cparams(("parallel", "parallel")), name="fourier_a",
    )(abv, m1a, m1b)
    yv = y.reshape(bsz, 2 * n1, n2, D_FOURIER)
    nblk = n1 // k1c
    out = pl.pallas_call(
        functools.partial(_fourier_b_kernel, k1c=k1c, n2=n2),
        grid=(bsz, nblk),
        in_specs=[
            pl.BlockSpec((1, k1c, n2, D_FOURIER), lambda b, c: (b, c, 0, 0)),
            pl.BlockSpec((1, k1c, n2, D_FOURIER), lambda b, c: (b, nblk + c, 0, 0)),
            pl.BlockSpec((k1c, n2, 2 * n2), lambda b, c: (c, 0, 0)),
        ],
        out_specs=pl.BlockSpec((1, n2, k1c, D_FOURIER), lambda b, c: (b, 0, c, 0)),
        out_shape=jax.ShapeDtypeStruct((bsz, n2, n1, D_FOURIER), jnp.float32),
        compiler_params=_cparams(("parallel", "parallel")), name="fourier_b",
    )(yv, yv, m2)
    return out.reshape(bsz, seq, D_FOURIER)


def _dft_tables(seq):
    n1 = DFT_N1
    n2 = seq // n1
    c = np.arange(FOURIER_GROUP)
    ang = 2.0 * np.pi * np.outer(c, c) / FOURIER_GROUP
    eye = np.eye(D_FOURIER // FOURIER_GROUP)
    cc = np.kron(eye, np.cos(ang))
    sc = np.kron(eye, np.sin(ang))
    i1 = np.arange(n1)
    a1 = 2.0 * np.pi * np.outer(i1, i1) / n1
    c1, s1 = np.cos(a1), np.sin(a1)
    m1a = np.concatenate([c1, -s1], axis=0)
    m1b = np.concatenate([-s1, -c1], axis=0)
    k = i1[:, None] + n1 * np.arange(n2)[None, :]
    phi = 2.0 * np.pi * (k[:, :, None] * np.arange(n2)[None, None, :] % seq) / seq
    norm = 1.0 / math.sqrt(seq * FOURIER_GROUP)
    m2 = np.concatenate([np.cos(phi), np.sin(phi)], axis=-1) * norm
    bf = lambda t: jnp.asarray(t, jnp.float32).astype(jnp.bfloat16)
    return bf(cc), bf(sc), bf(m1a), bf(m1b), bf(m2)


def _outproj_kernel(o_ref, u_ref, up_ref, un_ref, b_ref, f_ref, h_ref, wom_ref, woc_ref, wof_ref, wconv_ref,
                    g_ref, beta_ref, wr_ref, br_ref, h1_ref, comb_ref, *, alpha):
    i = pl.program_id(1)
    last = pl.num_programs(1) - 1
    tm = u_ref.shape[1]
    u = u_ref[0]
    up = up_ref[0][7:8, :] * (i > 0).astype(jnp.float32)
    un = un_ref[0][0:1, :] * (i < last).astype(jnp.float32)
    row = lax.broadcasted_iota(jnp.int32, u.shape, 0)
    u_m1 = jnp.where(row == 0, up, pltpu.roll(u, 1, axis=0))
    u_p1 = jnp.where(row == tm - 1, un, pltpu.roll(u, tm - 1, axis=0))
    wc = wconv_ref[...]
    oconv = b_ref[0] * (u_m1 * wc[0:1, :] + u * wc[1:2, :] + u_p1 * wc[2:3, :])
    mix = _bdot(oconv.astype(jnp.bfloat16), woc_ref[...]) + _bdot(f_ref[0].astype(jnp.bfloat16), wof_ref[...])
    for hd in range(N_HEADS):
        mix = mix + _bdot(o_ref[0, hd], wom_ref[hd])
    h1 = _layer_norm(alpha * h_ref[0] + mix, g_ref[...], beta_ref[...])
    h1_ref[0] = h1

    logits = jnp.dot(h1, wr_ref[...], preferred_element_type=jnp.float32,
                     precision=lax.Precision.HIGHEST) + br_ref[...]
    lane = lax.broadcasted_iota(jnp.int32, logits.shape, 1)
    neg = jnp.float32(-jnp.inf)
    big = jnp.int32(1 << 20)
    is_g = lane < N_GROUPS
    gl = jnp.where(is_g, logits, neg)
    gmax = jnp.max(gl, axis=-1, keepdims=True)
    p_group = 1.0 / jnp.sum(jnp.exp(gl - gmax), axis=-1, keepdims=True)
    g_sel = jnp.min(jnp.where(is_g & (logits == gmax), lane, big), axis=-1, keepdims=True)
    e_lo = N_GROUPS + g_sel * EXPERTS_PER_GROUP
    is_e = (lane >= e_lo) & (lane < e_lo + EXPERTS_PER_GROUP)
    l1 = jnp.max(jnp.where(is_e, logits, neg), axis=-1, keepdims=True)
    i1 = jnp.min(jnp.where(is_e & (logits == l1), lane, big), axis=-1, keepdims=True)
    is_e2 = is_e & (lane != i1)
    l2 = jnp.max(jnp.where(is_e2, logits, neg), axis=-1, keepdims=True)
    i2 = jnp.min(jnp.where(is_e2 & (logits == l2), lane, big), axis=-1, keepdims=True)
    r = jnp.exp(l2 - l1)
    w1 = 1.0 / (1.0 + r)
    w2 = r * w1
    comb_ref[0] = jnp.where(lane == i1, w1, jnp.where(lane == i2, w2, 0.0)) * p_group


def _outproj(o, u, bgate, ofour, h, wom, woc, wof, wconv, g, beta, wr, br, *, alpha, tm):
    bsz, seq, d = h.shape
    nblk8 = seq // 8
    r8 = tm // 8
    const2 = lambda b, i: (0, 0)
    const3 = lambda b, i: (0, 0, 0)
    tok = lambda w: pl.BlockSpec((1, tm, w), lambda b, i: (b, i, 0))
    return pl.pallas_call(
        functools.partial(_outproj_kernel, alpha=alpha),
        grid=(bsz, seq // tm),
        in_specs=[
            pl.BlockSpec((1, N_HEADS, tm, V_DIM), lambda b, i: (b, 0, i, 0)),
            tok(D_CONV),
            pl.BlockSpec((1, 8, D_CONV), lambda b, i: (b, jnp.maximum(i * r8 - 1, 0), 0)),
            pl.BlockSpec((1, 8, D_CONV), lambda b, i: (b, jnp.minimum((i + 1) * r8, nblk8 - 1), 0)),
            tok(D_CONV),
            tok(D_FOURIER),
            tok(d),
            pl.BlockSpec(wom.shape, const3),
            pl.BlockSpec(woc.shape, const2),
            pl.BlockSpec(wof.shape, const2),
            pl.BlockSpec(wconv.shape, const2),
            pl.BlockSpec((1, d), const2),
            pl.BlockSpec((1, d), const2),
            pl.BlockSpec(wr.shape, const2),
            pl.BlockSpec((1, ROUTER_LANES), const2),
        ],
        out_specs=[tok(d), tok(ROUTER_LANES)],
        out_shape=[jax.ShapeDtypeStruct((bsz, seq, d), jnp.float32),
                   jax.ShapeDtypeStruct((bsz, seq, ROUTER_LANES), jnp.float32)],
        compiler_params=_cparams(("parallel", "parallel")), name="outproj",
    )(o, u, u, u, bgate, ofour, h, wom, woc, wof, wconv, g, beta, wr, br)


def _moe_kernel(h_ref, comb_ref, wg_ref, wu_ref, wd_ref, g_ref, beta_ref, o_ref, xb_ref, acc_ref, *, alpha):
    e = pl.program_id(1)

    @pl.when(e == 0)
    def _():
        xb_ref[...] = h_ref[...].astype(xb_ref.dtype)
        acc_ref[...] = jnp.zeros_like(acc_ref)

    xb = xb_ref[...]
    a = _bdot(xb, wg_ref[0])
    up = _bdot(xb, wu_ref[0])
    lane = lax.broadcasted_iota(jnp.int32, comb_ref.shape, 1)
    cw = jnp.sum(jnp.where(lane == e + N_GROUPS, comb_ref[...], 0.0), axis=-1, keepdims=True)
    hid = a * (1.0 / (1.0 + jnp.exp(-a))) * up * cw
    acc_ref[...] += _bdot(hid.astype(jnp.bfloat16), wd_ref[0])

    @pl.when(e == pl.num_programs(1) - 1)
    def _():
        o_ref[...] = _layer_norm(alpha * h_ref[...] + acc_ref[...], g_ref[...], beta_ref[...])


def _moe(h, comb, wg, wu, wd, g, beta, *, alpha, tm):
    n, d = h.shape
    const2 = lambda i, e: (0, 0)
    return pl.pallas_call(
        functools.partial(_moe_kernel, alpha=alpha),
        grid=(n // tm, N_EXPERTS),
        in_specs=[
            pl.BlockSpec((tm, d), lambda i, e: (i, 0)),
            pl.BlockSpec((tm, ROUTER_LANES), lambda i, e: (i, 0)),
            pl.BlockSpec((1, d, D_EXPERT), lambda i, e: (e, 0, 0)),
            pl.BlockSpec((1, d, D_EXPERT), lambda i, e: (e, 0, 0)),
            pl.BlockSpec((1, D_EXPERT, d), lambda i, e: (e, 0, 0)),
            pl.BlockSpec((1, d), const2),
            pl.BlockSpec((1, d), const2),
        ],
        out_specs=pl.BlockSpec((tm, d), lambda i, e: (i, 0)),
        out_shape=jax.ShapeDtypeStruct((n, d), jnp.float32),
        scratch_shapes=[pltpu.VMEM((tm, d), jnp.bfloat16), pltpu.VMEM((tm, d), jnp.float32)],
        compiler_params=_cparams(("parallel", "arbitrary")), name="moe",
    )(h, comb, wg, wu, wd, g, beta)


def _rope_tables(seq):
    inv_freq = 1.0 / (ROPE_THETA ** (jnp.arange(0, QK_ROPE, 2, dtype=jnp.float32) / QK_ROPE))
    ang = jnp.arange(seq, dtype=jnp.float32)[:, None] * inv_freq[None, :]
    cos, sin = jnp.cos(ang), jnp.sin(ang)
    c96 = jnp.concatenate([jnp.ones((seq, QK_NOPE), jnp.float32), cos, cos], axis=-1)
    s96 = jnp.concatenate([jnp.zeros((seq, QK_NOPE), jnp.float32), -sin, sin], axis=-1)
    return c96, s96


def _swap_halves(w):
    half = w.shape[-1] // 2
    return jnp.concatenate([w[..., half:], w[..., :half]], axis=-1)


def _pad_cols(w, left, total):
    return jnp.pad(w, [(0, 0)] * (w.ndim - 1) + [(left, total - left - w.shape[-1])])


def _pick_tile(n, want):
    t = min(n, want)
    while n % t:
        t //= 2
    return t


def kernel(x, ln_in_g, ln_in_b, w_in, g_q, g_kv, w_uq, w_ukv, w_conv, w_out, ln1_g, ln1_b, w_group, b_group,
           w_router, b_router, w_gate, w_up, w_down, ln2_g, ln2_b):
    bsz, seq, d = x.shape
    depth = w_in.shape[0]
    alpha = (2.0 * depth) ** 0.25
    bf16 = jnp.bfloat16
    tm = _pick_tile(seq, 512)
    tq = _pick_tile(seq, 512)
    tk = _pick_tile(seq, 512)
    tm_moe = _pick_tile(bsz * seq, 1024)

    c96, s96 = _rope_tables(seq)
    cc, sc, m1a, m1b, m2 = _dft_tables(seq)
    row = lambda v: v.reshape(1, -1)
    d_mla = N_HEADS * V_DIM

    h = x
    for l in range(depth):
        wi = w_in[l]
        splits = np.cumsum([Q_LORA, KV_LORA, QK_ROPE, D_CONV, D_CONV, D_CONV])
        w_cq, w_ckv, w_kpe, w_b, w_c, w_h, w_f = jnp.split(wi, splits, axis=-1)
        win = jnp.concatenate([
            w_cq, w_ckv, w_b, w_c, w_h, w_f,
            _pad_cols(w_kpe, QK_NOPE, 128), _pad_cols(_swap_halves(w_kpe), QK_NOPE, 128)], axis=-1).astype(bf16)
        wq = jnp.transpose(w_uq[l], (1, 0, 2))
        wqs = _pad_cols(_swap_halves(wq[..., QK_NOPE:]), QK_NOPE, D_QK)
        wkv = jnp.transpose(w_ukv[l], (1, 0, 2))
        wk = _pad_cols(wkv[..., :QK_NOPE], 0, D_QK)
        wv = wkv[..., QK_NOPE:]
        outs = _inproj(h, row(ln_in_g), row(ln_in_b), win, row(g_q[l]), row(g_kv[l]), wq.astype(bf16),
                       wqs.astype(bf16), wk.astype(bf16), wv.astype(bf16), c96, s96, cc, sc,
                       ln_input=(l == 0), tm=tm)
        if l == 0:
            h, outs = outs[0], outs[1:]
        q, k, v, u, bgate, ab = outs
        o_mla = _attention(q, k, v, tq=tq, tk=tk)
        o_four = _fourier(ab, m1a, m1b, m2)

        wo = w_out[l].astype(bf16)
        wom = wo[:d_mla].reshape(N_HEADS, V_DIM, d)
        woc = wo[d_mla:d_mla + D_CONV]
        wof = wo[d_mla + D_CONV:]
        wr = _pad_cols(jnp.concatenate([w_group[l], w_router[l].reshape(d, N_EXPERTS)], axis=-1), 0, ROUTER_LANES)
        br = _pad_cols(jnp.concatenate([b_group[l], b_router[l].reshape(N_EXPERTS)])[None, :], 0, ROUTER_LANES)
        h1, comb = _outproj(o_mla, u, bgate, o_four, h, wom, woc, wof, w_conv[l], row(ln1_g[l]), row(ln1_b[l]),
                            wr, br, alpha=alpha, tm=tm)

        wg = w_gate[l].reshape(N_EXPERTS, d, D_EXPERT).astype(bf16)
        wu = w_up[l].reshape(N_EXPERTS, d, D_EXPERT).astype(bf16)
        wd = w_down[l].reshape(N_EXPERTS, D_EXPERT, d).astype(bf16)
        h = _moe(h1.reshape(bsz * seq, d), comb.reshape(bsz * seq, ROUTER_LANES), wg, wu, wd,
                 row(ln2_g[l]), row(ln2_b[l]), alpha=alpha, tm=tm_moe).reshape(bsz, seq, d)
    return h
```

```python
import functools
import math

import numpy as np
import jax
import jax.numpy as jnp
from jax import lax
from jax.experimental import pallas as pl
from jax.experimental.pallas import tpu as pltpu

N_HEADS = 8
QK_NOPE = 64
QK_ROPE = 32
D_QK = QK_NOPE + QK_ROPE
V_DIM = 64
V_ROWS = 80
Q_LORA = 384
KV_LORA = 256
D_CONV = 256
D_FOURIER = 256
FOURIER_GROUP = 64
N_GROUPS = 4
EXPERTS_PER_GROUP = 8
N_EXPERTS = N_GROUPS * EXPERTS_PER_GROUP
D_EXPERT = 256
ROPE_THETA = 10000.0
LN_EPS = 1e-5
RMS_EPS = 1e-6
DFT_N1 = 128
ROUTER_LANES = 128
VMEM_LIMIT = 56 * 1024 * 1024

_O_CQ = 0
_O_CKV = _O_CQ + Q_LORA
_O_B = _O_CKV + KV_LORA
_O_C = _O_B + D_CONV
_O_H = _O_C + D_CONV
_O_F = _O_H + D_CONV
_O_KPE = _O_F + D_FOURIER
_O_KPES = _O_KPE + 128
_W_IN_COLS = _O_KPES + 128


def _cparams(sem):
    return pltpu.CompilerParams(dimension_semantics=sem, vmem_limit_bytes=VMEM_LIMIT)


def _layer_norm(x, g, b):
    mu = jnp.mean(x, axis=-1, keepdims=True)
    xc = x - mu
    var = jnp.mean(xc * xc, axis=-1, keepdims=True)
    return xc * lax.rsqrt(var + LN_EPS) * g + b


def _rms_norm(x, g):
    return x * lax.rsqrt(jnp.mean(x * x, axis=-1, keepdims=True) + RMS_EPS) * g


def _bdot(a, b):
    return jnp.dot(a, b, preferred_element_type=jnp.float32)


def _dot_nt(a, b):
    return lax.dot_general(a, b, (((1,), (1,)), ((), ())), preferred_element_type=jnp.float32)


def _inproj_kernel(x_ref, lng_ref, lnb_ref, win_ref, gq_ref, gkv_ref, wqt_ref, wqst_ref, wk_ref, wvt_ref,
                   c96_ref, s96_ref, c96t_ref, s96t_ref, cc_ref, sc_ref, *out_refs, ln_input, q_scale):
    if ln_input:
        h_ref, qt_ref, k_ref, vt_ref, u_ref, b_ref, ab_ref = out_refs
        h = _layer_norm(x_ref[0], lng_ref[...], lnb_ref[...])
        h_ref[0] = h
    else:
        qt_ref, k_ref, vt_ref, u_ref, b_ref, ab_ref = out_refs
        h = x_ref[0]
    z = _bdot(h.astype(jnp.bfloat16), win_ref[...])
    c96 = c96_ref[...]
    s96 = s96_ref[...]
    c96t = c96t_ref[...]
    s96t = s96t_ref[...]
    cqn = _rms_norm(z[:, _O_CQ:_O_CQ + Q_LORA], gq_ref[...]).astype(jnp.bfloat16)
    ckvn = _rms_norm(z[:, _O_CKV:_O_CKV + KV_LORA], gkv_ref[...]).astype(jnp.bfloat16)
    kpe = z[:, _O_KPE:_O_KPE + D_QK] * c96 + z[:, _O_KPES:_O_KPES + D_QK] * s96
    qt_all = _dot_nt(wqt_ref[...], cqn)
    qst_all = _dot_nt(wqst_ref[...], cqn)
    vt_all = _dot_nt(wvt_ref[...], ckvn)
    vrow = lax.broadcasted_iota(jnp.int32, (V_ROWS, vt_all.shape[1]), 0)
    for hd in range(N_HEADS):
        qt = (qt_all[hd * D_QK:(hd + 1) * D_QK] * c96t + qst_all[hd * D_QK:(hd + 1) * D_QK] * s96t) * q_scale
        qt_ref[0, hd] = qt.astype(qt_ref.dtype)
        k_ref[0, hd] = (_bdot(ckvn, wk_ref[hd]) + kpe).astype(k_ref.dtype)
        vt = jnp.where(vrow == V_DIM, 1.0, vt_all[hd * V_ROWS:(hd + 1) * V_ROWS])
        vt_ref[0, hd, 0] = vt.astype(vt_ref.dtype)
    u_ref[0] = z[:, _O_C:_O_C + D_CONV] * z[:, _O_H:_O_H + D_CONV]
    b_ref[0] = z[:, _O_B:_O_B + D_CONV]
    fb = z[:, _O_F:_O_F + D_FOURIER].astype(jnp.bfloat16)
    ab_ref[0, :, :D_FOURIER] = _bdot(fb, cc_ref[...]).astype(ab_ref.dtype)
    ab_ref[0, :, D_FOURIER:] = _bdot(fb, sc_ref[...]).astype(ab_ref.dtype)


def _inproj(x, lng, lnb, win, gq, gkv, wqt, wqst, wk, wvt, c96, s96, c96t, s96t, cc, sc, *, ln_input, tm):
    bsz, seq, d = x.shape
    q_scale = (D_QK ** -0.5) * math.log2(math.e)
    grid = (bsz, seq // tm)
    const2 = lambda b, i: (0, 0)
    const3 = lambda b, i: (0, 0, 0)
    in_specs = [
        pl.BlockSpec((1, tm, d), lambda b, i: (b, i, 0)),
        pl.BlockSpec((1, d), const2),
        pl.BlockSpec((1, d), const2),
        pl.BlockSpec(win.shape, const2),
        pl.BlockSpec((1, Q_LORA), const2),
        pl.BlockSpec((1, KV_LORA), const2),
        pl.BlockSpec(wqt.shape, const2),
        pl.BlockSpec(wqst.shape, const2),
        pl.BlockSpec(wk.shape, const3),
        pl.BlockSpec(wvt.shape, const2),
        pl.BlockSpec((tm, D_QK), lambda b, i: (i, 0)),
        pl.BlockSpec((tm, D_QK), lambda b, i: (i, 0)),
        pl.BlockSpec((D_QK, tm), lambda b, i: (0, i)),
        pl.BlockSpec((D_QK, tm), lambda b, i: (0, i)),
        pl.BlockSpec(cc.shape, const2),
        pl.BlockSpec(sc.shape, const2),
    ]
    tok_spec = lambda w: pl.BlockSpec((1, tm, w), lambda b, i: (b, i, 0))
    out_shape = [
        jax.ShapeDtypeStruct((bsz, N_HEADS, D_QK, seq), jnp.bfloat16),
        jax.ShapeDtypeStruct((bsz, N_HEADS, seq, D_QK), jnp.bfloat16),
        jax.ShapeDtypeStruct((bsz, N_HEADS, seq // tm, V_ROWS, tm), jnp.bfloat16),
        jax.ShapeDtypeStruct((bsz, seq, D_CONV), jnp.float32),
        jax.ShapeDtypeStruct((bsz, seq, D_CONV), jnp.float32),
        jax.ShapeDtypeStruct((bsz, seq, 2 * D_FOURIER), jnp.bfloat16),
    ]
    out_specs = [pl.BlockSpec((1, N_HEADS, D_QK, tm), lambda b, i: (b, 0, 0, i)),
                 pl.BlockSpec((1, N_HEADS, tm, D_QK), lambda b, i: (b, 0, i, 0)),
                 pl.BlockSpec((1, N_HEADS, 1, V_ROWS, tm), lambda b, i: (b, 0, i, 0, 0)),
                 tok_spec(D_CONV), tok_spec(D_CONV), tok_spec(2 * D_FOURIER)]
    if ln_input:
        out_shape = [jax.ShapeDtypeStruct((bsz, seq, d), jnp.float32)] + out_shape
        out_specs = [tok_spec(d)] + out_specs
    return pl.pallas_call(
        functools.partial(_inproj_kernel, ln_input=ln_input, q_scale=q_scale),
        grid=grid, in_specs=in_specs, out_specs=out_specs, out_shape=out_shape,
        compiler_params=_cparams(("parallel", "parallel")), name="inproj",
    )(x, lng, lnb, win, gq, gkv, wqt, wqst, wk, wvt, c96, s96, c96t, s96t, cc, sc)


def _attn_kernel(qt_ref, k_ref, vt_ref, o_ref, s0_ref, s1_ref, m_ref, acc_ref, *, tk, n_kv):
    qt = qt_ref[0, 0]

    def scores(j, s_ref):
        off = pl.multiple_of(j * tk, tk)
        s_ref[...] = _bdot(k_ref[0, 0, pl.ds(off, tk), :], qt)

    def update(j, s_ref):
        s = s_ref[...]
        m_old = m_ref[...]
        m_new = jnp.maximum(m_old, jnp.max(s, axis=0, keepdims=True))
        p = jnp.exp2(s - m_new).astype(jnp.bfloat16)
        acc_ref[...] = jnp.exp2(m_old - m_new) * acc_ref[...] + _bdot(vt_ref[0, 0, j], p)
        m_ref[...] = m_new

    m_ref[...] = jnp.full(m_ref.shape, -jnp.inf, jnp.float32)
    acc_ref[...] = jnp.zeros(acc_ref.shape, jnp.float32)
    scores(0, s0_ref)

    def pair(i, carry):
        j = 2 * i
        scores(j + 1, s1_ref)
        update(j, s0_ref)
        scores(j + 2, s0_ref)
        update(j + 1, s1_ref)
        return carry

    lax.fori_loop(0, n_kv // 2 - 1, pair, 0)
    scores(n_kv - 1, s1_ref)
    update(n_kv - 2, s0_ref)
    update(n_kv - 1, s1_ref)

    acc = acc_ref[...]
    ot = acc[:V_DIM] / acc[V_DIM:V_DIM + 1]
    o_ref[0, 0] = ot.T.astype(o_ref.dtype)


def _attention(qt, k, vt, *, tq, tk):
    bsz, nh, seq, _ = k.shape
    n_kv = seq // tk
    assert n_kv % 2 == 0 and vt.shape[2:] == (n_kv, V_ROWS, tk)
    return pl.pallas_call(
        functools.partial(_attn_kernel, tk=tk, n_kv=n_kv),
        grid=(bsz, nh, seq // tq),
        in_specs=[
            pl.BlockSpec((1, 1, D_QK, tq), lambda b, h, i: (b, h, 0, i)),
            pl.BlockSpec((1, 1, seq, D_QK), lambda b, h, i: (b, h, 0, 0)),
            pl.BlockSpec((1, 1, n_kv, V_ROWS, tk), lambda b, h, i: (b, h, 0, 0, 0)),
        ],
        out_specs=pl.BlockSpec((1, 1, tq, V_DIM), lambda b, h, i: (b, h, i, 0)),
        out_shape=jax.ShapeDtypeStruct((bsz, nh, seq, V_DIM), jnp.bfloat16),
        scratch_shapes=[pltpu.VMEM((tk, tq), jnp.float32), pltpu.VMEM((tk, tq), jnp.float32),
                        pltpu.VMEM((1, tq), jnp.float32), pltpu.VMEM((V_ROWS, tq), jnp.float32)],
        compiler_params=_cparams(("parallel", "parallel", "arbitrary")), name="attention",
    )(qt, k, vt)


def _fourier_a_kernel(ab_ref, m1a_ref, m1b_ref, y_ref, *, n2c):
    for j in range(n2c):
        a = ab_ref[0, :, j * 2 * D_FOURIER: j * 2 * D_FOURIER + D_FOURIER]
        b = ab_ref[0, :, j * 2 * D_FOURIER + D_FOURIER: (j + 1) * 2 * D_FOURIER]
        y = _bdot(m1a_ref[...], a) + _bdot(m1b_ref[...], b)
        y_ref[0, :, j * D_FOURIER:(j + 1) * D_FOURIER] = y.astype(y_ref.dtype)


def _fourier_b_kernel(yr_ref, yi_ref, m2_ref, o_ref, *, k1c, n2):
    for j in range(k1c):
        m2 = m2_ref[j]
        z = _bdot(m2[:, :n2], yr_ref[0, j]) + _bdot(m2[:, n2:], yi_ref[0, j])
        o_ref[0, :, j, :] = z.astype(o_ref.dtype)


def _fourier(ab, m1a, m1b, m2):
    bsz, seq, _ = ab.shape
    n1 = DFT_N1
    n2 = seq // n1
    n2c = min(n2, 16)
    k1c = 8
    abv = ab.reshape(bsz, n1, n2 * 2 * D_FOURIER)
    y = pl.pallas_call(
        functools.partial(_fourier_a_kernel, n2c=n2c),
        grid=(bsz, n2 // n2c),
        in_specs=[
            pl.BlockSpec((1, n1, n2c * 2 * D_FOURIER), lambda b, c: (b, 0, c)),
            pl.BlockSpec(m1a.shape, lambda b, c: (0, 0)),
            pl.BlockSpec(m1b.shape, lambda b, c: (0, 0)),
        ],
        out_specs=pl.BlockSpec((1, 2 * n1, n2c * D_FOURIER), lambda b, c: (b, 0, c)),
        out_shape=jax.ShapeDtypeStruct((bsz, 2 * n1, n2 * D_FOURIER), jnp.bfloat16),
        compiler_params=_cparams(("parallel", "parallel")), name="fourier_a",
    )(abv, m1a, m1b)
    yv = y.reshape(bsz, 2 * n1, n2, D_FOURIER)
    nblk = n1 // k1c
    out = pl.pallas_call(
        functools.partial(_fourier_b_kernel, k1c=k1c, n2=n2),
        grid=(bsz, nblk),
        in_specs=[
            pl.BlockSpec((1, k1c, n2, D_FOURIER), lambda b, c: (b, c, 0, 0)),
            pl.BlockSpec((1, k1c, n2, D_FOURIER), lambda b, c: (b, nblk + c, 0, 0)),
            pl.BlockSpec((k1c, n2, 2 * n2), lambda b, c: (c, 0, 0)),
        ],
        out_specs=pl.BlockSpec((1, n2, k1c, D_FOURIER), lambda b, c: (b, 0, c, 0)),
        out_shape=jax.ShapeDtypeStruct((bsz, n2, n1, D_FOURIER), jnp.float32),
        compiler_params=_cparams(("parallel", "parallel")), name="fourier_b",
    )(yv, yv, m2)
    return out.reshape(bsz, seq, D_FOURIER)


def _dft_tables(seq):
    n1 = DFT_N1
    n2 = seq // n1
    c = np.arange(FOURIER_GROUP)
    ang = 2.0 * np.pi * np.outer(c, c) / FOURIER_GROUP
    eye = np.eye(D_FOURIER // FOURIER_GROUP)
    cc = np.kron(eye, np.cos(ang))
    sc = np.kron(eye, np.sin(ang))
    i1 = np.arange(n1)
    a1 = 2.0 * np.pi * np.outer(i1, i1) / n1
    c1, s1 = np.cos(a1), np.sin(a1)
    m1a = np.concatenate([c1, -s1], axis=0)
    m1b = np.concatenate([-s1, -c1], axis=0)
    k = i1[:, None] + n1 * np.arange(n2)[None, :]
    phi = 2.0 * np.pi * (k[:, :, None] * np.arange(n2)[None, None, :] % seq) / seq
    norm = 1.0 / math.sqrt(seq * FOURIER_GROUP)
    m2 = np.concatenate([np.cos(phi), np.sin(phi)], axis=-1) * norm
    bf = lambda t: jnp.asarray(t, jnp.float32).astype(jnp.bfloat16)
    return bf(cc), bf(sc), bf(m1a), bf(m1b), bf(m2)


def _outproj_kernel(o_ref, u_ref, up_ref, un_ref, b_ref, f_ref, h_ref, wom_ref, woc_ref, wof_ref, wconv_ref,
                    g_ref, beta_ref, wr_ref, br_ref, h1_ref, comb_ref, *, alpha):
    i = pl.program_id(1)
    last = pl.num_programs(1) - 1
    tm = u_ref.shape[1]
    u = u_ref[0]
    up = up_ref[0][7:8, :] * (i > 0).astype(jnp.float32)
    un = un_ref[0][0:1, :] * (i < last).astype(jnp.float32)
    row = lax.broadcasted_iota(jnp.int32, u.shape, 0)
    u_m1 = jnp.where(row == 0, up, pltpu.roll(u, 1, axis=0))
    u_p1 = jnp.where(row == tm - 1, un, pltpu.roll(u, tm - 1, axis=0))
    wc = wconv_ref[...]
    oconv = b_ref[0] * (u_m1 * wc[0:1, :] + u * wc[1:2, :] + u_p1 * wc[2:3, :])
    mix = _bdot(oconv.astype(jnp.bfloat16), woc_ref[...]) + _bdot(f_ref[0].astype(jnp.bfloat16), wof_ref[...])
    for hd in range(N_HEADS):
        mix = mix + _bdot(o_ref[0, hd], wom_ref[hd])
    h1 = _layer_norm(alpha * h_ref[0] + mix, g_ref[...], beta_ref[...])
    h1_ref[0] = h1

    logits = jnp.dot(h1, wr_ref[...], preferred_element_type=jnp.float32,
                     precision=lax.Precision.HIGHEST) + br_ref[...]
    lane = lax.broadcasted_iota(jnp.int32, logits.shape, 1)
    neg = jnp.float32(-jnp.inf)
    big = jnp.int32(1 << 20)
    is_g = lane < N_GROUPS
    gl = jnp.where(is_g, logits, neg)
    gmax = jnp.max(gl, axis=-1, keepdims=True)
    p_group = 1.0 / jnp.sum(jnp.exp(gl - gmax), axis=-1, keepdims=True)
    g_sel = jnp.min(jnp.where(is_g & (logits == gmax), lane, big), axis=-1, keepdims=True)
    e_lo = N_GROUPS + g_sel * EXPERTS_PER_GROUP
    is_e = (lane >= e_lo) & (lane < e_lo + EXPERTS_PER_GROUP)
    l1 = jnp.max(jnp.where(is_e, logits, neg), axis=-1, keepdims=True)
    i1 = jnp.min(jnp.where(is_e & (logits == l1), lane, big), axis=-1, keepdims=True)
    is_e2 = is_e & (lane != i1)
    l2 = jnp.max(jnp.where(is_e2, logits, neg), axis=-1, keepdims=True)
    i2 = jnp.min(jnp.where(is_e2 & (logits == l2), lane, big), axis=-1, keepdims=True)
    r = jnp.exp(l2 - l1)
    w1 = 1.0 / (1.0 + r)
    w2 = r * w1
    comb_ref[0] = jnp.where(lane == i1, w1, jnp.where(lane == i2, w2, 0.0)) * p_group


def _outproj(o, u, bgate, ofour, h, wom, woc, wof, wconv, g, beta, wr, br, *, alpha, tm):
    bsz, seq, d = h.shape
    nblk8 = seq // 8
    r8 = tm // 8
    const2 = lambda b, i: (0, 0)
    const3 = lambda b, i: (0, 0, 0)
    tok = lambda w: pl.BlockSpec((1, tm, w), lambda b, i: (b, i, 0))
    return pl.pallas_call(
        functools.partial(_outproj_kernel, alpha=alpha),
        grid=(bsz, seq // tm),
        in_specs=[
            pl.BlockSpec((1, N_HEADS, tm, V_DIM), lambda b, i: (b, 0, i, 0)),
            tok(D_CONV),
            pl.BlockSpec((1, 8, D_CONV), lambda b, i: (b, jnp.maximum(i * r8 - 1, 0), 0)),
            pl.BlockSpec((1, 8, D_CONV), lambda b, i: (b, jnp.minimum((i + 1) * r8, nblk8 - 1), 0)),
            tok(D_CONV),
            tok(D_FOURIER),
            tok(d),
            pl.BlockSpec(wom.shape, const3),
            pl.BlockSpec(woc.shape, const2),
            pl.BlockSpec(wof.shape, const2),
            pl.BlockSpec(wconv.shape, const2),
            pl.BlockSpec((1, d), const2),
            pl.BlockSpec((1, d), const2),
            pl.BlockSpec(wr.shape, const2),
            pl.BlockSpec((1, ROUTER_LANES), const2),
        ],
        out_specs=[tok(d), tok(ROUTER_LANES)],
        out_shape=[jax.ShapeDtypeStruct((bsz, seq, d), jnp.float32),
                   jax.ShapeDtypeStruct((bsz, seq, ROUTER_LANES), jnp.float32)],
        compiler_params=_cparams(("parallel", "parallel")), name="outproj",
    )(o, u, u, u, bgate, ofour, h, wom, woc, wof, wconv, g, beta, wr, br)


def _moe_kernel(h_ref, comb_ref, wg_ref, wu_ref, wd_ref, g_ref, beta_ref, o_ref, xb_ref, acc_ref, *, alpha):
    e = pl.program_id(1)

    @pl.when(e == 0)
    def _():
        xb_ref[...] = h_ref[...].astype(xb_ref.dtype)
        acc_ref[...] = jnp.zeros_like(acc_ref)

    xb = xb_ref[...]
    a = _bdot(xb, wg_ref[0])
    up = _bdot(xb, wu_ref[0])
    lane = lax.broadcasted_iota(jnp.int32, comb_ref.shape, 1)
    cw = jnp.sum(jnp.where(lane == e + N_GROUPS, comb_ref[...], 0.0), axis=-1, keepdims=True)
    hid = a * (1.0 / (1.0 + jnp.exp(-a))) * up * cw
    acc_ref[...] += _bdot(hid.astype(jnp.bfloat16), wd_ref[0])

    @pl.when(e == pl.num_programs(1) - 1)
    def _():
        o_ref[...] = _layer_norm(alpha * h_ref[...] + acc_ref[...], g_ref[...], beta_ref[...])


def _moe(h, comb, wg, wu, wd, g, beta, *, alpha, tm):
    n, d = h.shape
    const2 = lambda i, e: (0, 0)
    return pl.pallas_call(
        functools.partial(_moe_kernel, alpha=alpha),
        grid=(n // tm, N_EXPERTS),
        in_specs=[
            pl.BlockSpec((tm, d), lambda i, e: (i, 0)),
            pl.BlockSpec((tm, ROUTER_LANES), lambda i, e: (i, 0)),
            pl.BlockSpec((1, d, D_EXPERT), lambda i, e: (e, 0, 0)),
            pl.BlockSpec((1, d, D_EXPERT), lambda i, e: (e, 0, 0)),
            pl.BlockSpec((1, D_EXPERT, d), lambda i, e: (e, 0, 0)),
            pl.BlockSpec((1, d), const2),
            pl.BlockSpec((1, d), const2),
        ],
        out_specs=pl.BlockSpec((tm, d), lambda i, e: (i, 0)),
        out_shape=jax.ShapeDtypeStruct((n, d), jnp.float32),
        scratch_shapes=[pltpu.VMEM((tm, d), jnp.bfloat16), pltpu.VMEM((tm, d), jnp.float32)],
        compiler_params=_cparams(("parallel", "arbitrary")), name="moe",
    )(h, comb, wg, wu, wd, g, beta)


def _rope_tables(seq):
    inv_freq = 1.0 / (ROPE_THETA ** (jnp.arange(0, QK_ROPE, 2, dtype=jnp.float32) / QK_ROPE))
    ang = jnp.arange(seq, dtype=jnp.float32)[:, None] * inv_freq[None, :]
    cos, sin = jnp.cos(ang), jnp.sin(ang)
    c96 = jnp.concatenate([jnp.ones((seq, QK_NOPE), jnp.float32), cos, cos], axis=-1)
    s96 = jnp.concatenate([jnp.zeros((seq, QK_NOPE), jnp.float32), -sin, sin], axis=-1)
    return c96, s96, c96.T, s96.T


def _swap_halves(w):
    half = w.shape[-1] // 2
    return jnp.concatenate([w[..., half:], w[..., :half]], axis=-1)


def _pad_cols(w, left, total):
    return jnp.pad(w, [(0, 0)] * (w.ndim - 1) + [(left, total - left - w.shape[-1])])


def _pick_tile(n, want):
    t = min(n, want)
    while n % t:
        t //= 2
    return t


def kernel(x, ln_in_g, ln_in_b, w_in, g_q, g_kv, w_uq, w_ukv, w_conv, w_out, ln1_g, ln1_b, w_group, b_group,
           w_router, b_router, w_gate, w_up, w_down, ln2_g, ln2_b):
    bsz, seq, d = x.shape
    depth = w_in.shape[0]
    alpha = (2.0 * depth) ** 0.25
    bf16 = jnp.bfloat16
    tm = _pick_tile(seq, 512)
    tq = _pick_tile(seq, 1024)
    tk = tm
    tm_moe = _pick_tile(bsz * seq, 1024)

    c96, s96, c96t, s96t = _rope_tables(seq)
    cc, sc, m1a, m1b, m2 = _dft_tables(seq)
    row = lambda v: v.reshape(1, -1)
    d_mla = N_HEADS * V_DIM

    h = x
    for l in range(depth):
        wi = w_in[l]
        splits = np.cumsum([Q_LORA, KV_LORA, QK_ROPE, D_CONV, D_CONV, D_CONV])
        w_cq, w_ckv, w_kpe, w_b, w_c, w_h, w_f = jnp.split(wi, splits, axis=-1)
        win = jnp.concatenate([
            w_cq, w_ckv, w_b, w_c, w_h, w_f,
            _pad_cols(w_kpe, QK_NOPE, 128), _pad_cols(_swap_halves(w_kpe), QK_NOPE, 128)], axis=-1).astype(bf16)
        wq = jnp.transpose(w_uq[l], (1, 0, 2))
        wqs = _pad_cols(_swap_halves(wq[..., QK_NOPE:]), QK_NOPE, D_QK)
        wqt = jnp.transpose(wq, (0, 2, 1)).reshape(N_HEADS * D_QK, Q_LORA)
        wqst = jnp.transpose(wqs, (0, 2, 1)).reshape(N_HEADS * D_QK, Q_LORA)
        wkv = jnp.transpose(w_ukv[l], (1, 0, 2))
        wk = _pad_cols(wkv[..., :QK_NOPE], 0, D_QK)
        wvt = _pad_cols(wkv[..., QK_NOPE:], 0, V_ROWS)
        wvt = jnp.transpose(wvt, (0, 2, 1)).reshape(N_HEADS * V_ROWS, KV_LORA)
        outs = _inproj(h, row(ln_in_g), row(ln_in_b), win, row(g_q[l]), row(g_kv[l]), wqt.astype(bf16),
                       wqst.astype(bf16), wk.astype(bf16), wvt.astype(bf16), c96, s96, c96t, s96t, cc, sc,
                       ln_input=(l == 0), tm=tm)
        if l == 0:
            h, outs = outs[0], outs[1:]
        qt, k, vt, u, bgate, ab = outs
        o_mla = _attention(qt, k, vt, tq=tq, tk=tk)
        o_four = _fourier(ab, m1a, m1b, m2)

        wo = w_out[l].astype(bf16)
        wom = wo[:d_mla].reshape(N_HEADS, V_DIM, d)
        woc = wo[d_mla:d_mla + D_CONV]
        wof = wo[d_mla + D_CONV:]
        wr = _pad_cols(jnp.concatenate([w_group[l], w_router[l].reshape(d, N_EXPERTS)], axis=-1), 0, ROUTER_LANES)
        br = _pad_cols(jnp.concatenate([b_group[l], b_router[l].reshape(N_EXPERTS)])[None, :], 0, ROUTER_LANES)
        h1, comb = _outproj(o_mla, u, bgate, o_four, h, wom, woc, wof, w_conv[l], row(ln1_g[l]), row(ln1_b[l]),
                            wr, br, alpha=alpha, tm=tm)

        wg = w_gate[l].reshape(N_EXPERTS, d, D_EXPERT).astype(bf16)
        wu = w_up[l].reshape(N_EXPERTS, d, D_EXPERT).astype(bf16)
        wd = w_down[l].reshape(N_EXPERTS, D_EXPERT, d).astype(bf16)
        h = _moe(h1.reshape(bsz * seq, d), comb.reshape(bsz * seq, ROUTER_LANES), wg, wu, wd,
                 row(ln2_g[l]), row(ln2_b[l]), alpha=alpha, tm=tm_moe).reshape(bsz, seq, d)
    return h
```

```python
import functools
import math

import numpy as np
import jax
import jax.numpy as jnp
from jax import lax
from jax.experimental import pallas as pl
from jax.experimental.pallas import tpu as pltpu

N_HEADS = 8
QK_NOPE = 64
QK_ROPE = 32
D_QK = QK_NOPE + QK_ROPE
K_LANES = 128
V_DIM = 64
V_ROWS = 80
Q_LORA = 384
KV_LORA = 256
D_CONV = 256
D_FOURIER = 256
FOURIER_GROUP = 64
N_GROUPS = 4
EXPERTS_PER_GROUP = 8
N_EXPERTS = N_GROUPS * EXPERTS_PER_GROUP
D_EXPERT = 256
ROPE_THETA = 10000.0
LN_EPS = 1e-5
RMS_EPS = 1e-6
DFT_N1 = 128
ROUTER_LANES = 128
VMEM_LIMIT = 56 * 1024 * 1024

_O_CQ = 0
_O_CKV = _O_CQ + Q_LORA
_O_B = _O_CKV + KV_LORA
_O_C = _O_B + D_CONV
_O_H = _O_C + D_CONV
_O_F = _O_H + D_CONV
_O_KPE = _O_F + D_FOURIER
_O_KPES = _O_KPE + 128
_W_IN_COLS = _O_KPES + 128


def _cparams(sem):
    return pltpu.CompilerParams(dimension_semantics=sem, vmem_limit_bytes=VMEM_LIMIT)


def _layer_norm(x, g, b):
    mu = jnp.mean(x, axis=-1, keepdims=True)
    xc = x - mu
    var = jnp.mean(xc * xc, axis=-1, keepdims=True)
    return xc * lax.rsqrt(var + LN_EPS) * g + b


def _rms_norm(x, g):
    return x * lax.rsqrt(jnp.mean(x * x, axis=-1, keepdims=True) + RMS_EPS) * g


def _bdot(a, b):
    return jnp.dot(a, b, preferred_element_type=jnp.float32)


def _dot_nt(a, b):
    return lax.dot_general(a, b, (((1,), (1,)), ((), ())), preferred_element_type=jnp.float32)


def _inproj_kernel(x_ref, lng_ref, lnb_ref, win_ref, gq_ref, gkv_ref, wqt_ref, wqst_ref, wk_ref, wvt_ref,
                   c96_ref, s96_ref, c96t_ref, s96t_ref, cc_ref, sc_ref, *out_refs, ln_input, q_scale):
    if ln_input:
        h_ref, qt_ref, k_ref, vt_ref, u_ref, b_ref, ab_ref = out_refs
        h = _layer_norm(x_ref[0], lng_ref[...], lnb_ref[...])
        h_ref[0] = h
    else:
        qt_ref, k_ref, vt_ref, u_ref, b_ref, ab_ref = out_refs
        h = x_ref[0]
    z = _bdot(h.astype(jnp.bfloat16), win_ref[...])
    c96 = c96_ref[...]
    s96 = s96_ref[...]
    c96t = c96t_ref[...]
    s96t = s96t_ref[...]
    cqn = _rms_norm(z[:, _O_CQ:_O_CQ + Q_LORA], gq_ref[...]).astype(jnp.bfloat16)
    ckvn = _rms_norm(z[:, _O_CKV:_O_CKV + KV_LORA], gkv_ref[...]).astype(jnp.bfloat16)
    kpe = z[:, _O_KPE:_O_KPE + K_LANES] * c96 + z[:, _O_KPES:_O_KPES + K_LANES] * s96
    qt_all = _dot_nt(wqt_ref[...], cqn)
    qst_all = _dot_nt(wqst_ref[...], cqn)
    vt_all = _dot_nt(wvt_ref[...], ckvn)
    vrow = lax.broadcasted_iota(jnp.int32, (V_ROWS, vt_all.shape[1]), 0)
    for hd in range(N_HEADS):
        qt = (qt_all[hd * D_QK:(hd + 1) * D_QK] * c96t + qst_all[hd * D_QK:(hd + 1) * D_QK] * s96t) * q_scale
        qt_ref[0, hd] = qt.astype(qt_ref.dtype)
        k_ref[0, hd] = (_bdot(ckvn, wk_ref[hd]) + kpe).astype(k_ref.dtype)
        vt = jnp.where(vrow == V_DIM, 1.0, vt_all[hd * V_ROWS:(hd + 1) * V_ROWS])
        vt_ref[0, hd, 0] = vt.astype(vt_ref.dtype)
    u_ref[0] = z[:, _O_C:_O_C + D_CONV] * z[:, _O_H:_O_H + D_CONV]
    b_ref[0] = z[:, _O_B:_O_B + D_CONV]
    fb = z[:, _O_F:_O_F + D_FOURIER].astype(jnp.bfloat16)
    ab_ref[0, :, :D_FOURIER] = _bdot(fb, cc_ref[...]).astype(ab_ref.dtype)
    ab_ref[0, :, D_FOURIER:] = _bdot(fb, sc_ref[...]).astype(ab_ref.dtype)


def _inproj(x, lng, lnb, win, gq, gkv, wqt, wqst, wk, wvt, c96, s96, c96t, s96t, cc, sc, *, ln_input, tm):
    bsz, seq, d = x.shape
    q_scale = (D_QK ** -0.5) * math.log2(math.e)
    grid = (bsz, seq // tm)
    const2 = lambda b, i: (0, 0)
    const3 = lambda b, i: (0, 0, 0)
    in_specs = [
        pl.BlockSpec((1, tm, d), lambda b, i: (b, i, 0)),
        pl.BlockSpec((1, d), const2),
        pl.BlockSpec((1, d), const2),
        pl.BlockSpec(win.shape, const2),
        pl.BlockSpec((1, Q_LORA), const2),
        pl.BlockSpec((1, KV_LORA), const2),
        pl.BlockSpec(wqt.shape, const2),
        pl.BlockSpec(wqst.shape, const2),
        pl.BlockSpec(wk.shape, const3),
        pl.BlockSpec(wvt.shape, const2),
        pl.BlockSpec((tm, K_LANES), lambda b, i: (i, 0)),
        pl.BlockSpec((tm, K_LANES), lambda b, i: (i, 0)),
        pl.BlockSpec((D_QK, tm), lambda b, i: (0, i)),
        pl.BlockSpec((D_QK, tm), lambda b, i: (0, i)),
        pl.BlockSpec(cc.shape, const2),
        pl.BlockSpec(sc.shape, const2),
    ]
    tok_spec = lambda w: pl.BlockSpec((1, tm, w), lambda b, i: (b, i, 0))
    out_shape = [
        jax.ShapeDtypeStruct((bsz, N_HEADS, D_QK, seq), jnp.bfloat16),
        jax.ShapeDtypeStruct((bsz, N_HEADS, seq, K_LANES), jnp.bfloat16),
        jax.ShapeDtypeStruct((bsz, N_HEADS, seq // tm, V_ROWS, tm), jnp.bfloat16),
        jax.ShapeDtypeStruct((bsz, seq, D_CONV), jnp.float32),
        jax.ShapeDtypeStruct((bsz, seq, D_CONV), jnp.float32),
        jax.ShapeDtypeStruct((bsz, seq, 2 * D_FOURIER), jnp.bfloat16),
    ]
    out_specs = [pl.BlockSpec((1, N_HEADS, D_QK, tm), lambda b, i: (b, 0, 0, i)),
                 pl.BlockSpec((1, N_HEADS, tm, K_LANES), lambda b, i: (b, 0, i, 0)),
                 pl.BlockSpec((1, N_HEADS, 1, V_ROWS, tm), lambda b, i: (b, 0, i, 0, 0)),
                 tok_spec(D_CONV), tok_spec(D_CONV), tok_spec(2 * D_FOURIER)]
    if ln_input:
        out_shape = [jax.ShapeDtypeStruct((bsz, seq, d), jnp.float32)] + out_shape
        out_specs = [tok_spec(d)] + out_specs
    return pl.pallas_call(
        functools.partial(_inproj_kernel, ln_input=ln_input, q_scale=q_scale),
        grid=grid, in_specs=in_specs, out_specs=out_specs, out_shape=out_shape,
        compiler_params=_cparams(("parallel", "parallel")), name="inproj",
    )(x, lng, lnb, win, gq, gkv, wqt, wqst, wk, wvt, c96, s96, c96t, s96t, cc, sc)


def _attn_kernel(qt_ref, k_ref, vt_ref, o_ref, s0_ref, s1_ref, m_ref, acc_ref, *, tk, n_kv):
    qt = qt_ref[0, 0]

    def scores(j, s_ref):
        off = pl.multiple_of(j * tk, tk)
        s_ref[...] = _bdot(k_ref[0, 0, pl.ds(off, tk), :D_QK], qt)

    def update(j, s_ref):
        s = s_ref[...]
        m_old = m_ref[...]
        m_new = jnp.maximum(m_old, jnp.max(s, axis=0, keepdims=True))
        p = jnp.exp2(s - m_new).astype(jnp.bfloat16)
        acc_ref[...] = jnp.exp2(m_old - m_new) * acc_ref[...] + _bdot(vt_ref[0, 0, j], p)
        m_ref[...] = m_new

    m_ref[...] = jnp.full(m_ref.shape, -jnp.inf, jnp.float32)
    acc_ref[...] = jnp.zeros(acc_ref.shape, jnp.float32)
    scores(0, s0_ref)

    def pair(i, carry):
        j = 2 * i
        scores(j + 1, s1_ref)
        update(j, s0_ref)
        scores(j + 2, s0_ref)
        update(j + 1, s1_ref)
        return carry

    lax.fori_loop(0, n_kv // 2 - 1, pair, 0)
    scores(n_kv - 1, s1_ref)
    update(n_kv - 2, s0_ref)
    update(n_kv - 1, s1_ref)

    acc = acc_ref[...]
    o_ref[0] = (acc[:V_DIM] / acc[V_DIM:V_DIM + 1]).astype(o_ref.dtype)


def _attention(qt, k, vt, *, tq, tk):
    bsz, nh, seq, _ = k.shape
    n_kv = seq // tk
    assert n_kv % 2 == 0 and vt.shape[2:] == (n_kv, V_ROWS, tk)
    return pl.pallas_call(
        functools.partial(_attn_kernel, tk=tk, n_kv=n_kv),
        grid=(bsz, nh, seq // tq),
        in_specs=[
            pl.BlockSpec((1, 1, D_QK, tq), lambda b, h, i: (b, h, 0, i)),
            pl.BlockSpec((1, 1, seq, K_LANES), lambda b, h, i: (b, h, 0, 0)),
            pl.BlockSpec((1, 1, n_kv, V_ROWS, tk), lambda b, h, i: (b, h, 0, 0, 0)),
        ],
        out_specs=pl.BlockSpec((1, V_DIM, tq), lambda b, h, i: (b, h, i)),
        out_shape=jax.ShapeDtypeStruct((bsz, nh * V_DIM, seq), jnp.bfloat16),
        scratch_shapes=[pltpu.VMEM((tk, tq), jnp.float32), pltpu.VMEM((tk, tq), jnp.float32),
                        pltpu.VMEM((1, tq), jnp.float32), pltpu.VMEM((V_ROWS, tq), jnp.float32)],
        compiler_params=_cparams(("parallel", "parallel", "arbitrary")), name="attention",
    )(qt, k, vt)


def _fourier_a_kernel(ab_ref, m1a_ref, m1b_ref, y_ref, *, n2c):
    for j in range(n2c):
        a = ab_ref[0, :, j * 2 * D_FOURIER: j * 2 * D_FOURIER + D_FOURIER]
        b = ab_ref[0, :, j * 2 * D_FOURIER + D_FOURIER: (j + 1) * 2 * D_FOURIER]
        y = _bdot(m1a_ref[...], a) + _bdot(m1b_ref[...], b)
        y_ref[0, :, j * D_FOURIER:(j + 1) * D_FOURIER] = y.astype(y_ref.dtype)


def _fourier_b_kernel(yr_ref, yi_ref, m2_ref, o_ref, *, k1c, n2):
    for j in range(k1c):
        m2 = m2_ref[j]
        z = _bdot(m2[:, :n2], yr_ref[0, j]) + _bdot(m2[:, n2:], yi_ref[0, j])
        o_ref[0, :, j, :] = z.astype(o_ref.dtype)


def _fourier(ab, m1a, m1b, m2):
    bsz, seq, _ = ab.shape
    n1 = DFT_N1
    n2 = seq // n1
    n2c = min(n2, 16)
    k1c = 8
    abv = ab.reshape(bsz, n1, n2 * 2 * D_FOURIER)
    y = pl.pallas_call(
        functools.partial(_fourier_a_kernel, n2c=n2c),
        grid=(bsz, n2 // n2c),
        in_specs=[
            pl.BlockSpec((1, n1, n2c * 2 * D_FOURIER), lambda b, c: (b, 0, c)),
            pl.BlockSpec(m1a.shape, lambda b, c: (0, 0)),
            pl.BlockSpec(m1b.shape, lambda b, c: (0, 0)),
        ],
        out_specs=pl.BlockSpec((1, 2 * n1, n2c * D_FOURIER), lambda b, c: (b, 0, c)),
        out_shape=jax.ShapeDtypeStruct((bsz, 2 * n1, n2 * D_FOURIER), jnp.bfloat16),
        compiler_params=_cparams(("parallel", "parallel")), name="fourier_a",
    )(abv, m1a, m1b)
    yv = y.reshape(bsz, 2 * n1, n2, D_FOURIER)
    nblk = n1 // k1c
    out = pl.pallas_call(
        functools.partial(_fourier_b_kernel, k1c=k1c, n2=n2),
        grid=(bsz, nblk),
        in_specs=[
            pl.BlockSpec((1, k1c, n2, D_FOURIER), lambda b, c: (b, c, 0, 0)),
            pl.BlockSpec((1, k1c, n2, D_FOURIER), lambda b, c: (b, nblk + c, 0, 0)),
            pl.BlockSpec((k1c, n2, 2 * n2), lambda b, c: (c, 0, 0)),
        ],
        out_specs=pl.BlockSpec((1, n2, k1c, D_FOURIER), lambda b, c: (b, 0, c, 0)),
        out_shape=jax.ShapeDtypeStruct((bsz, n2, n1, D_FOURIER), jnp.float32),
        compiler_params=_cparams(("parallel", "parallel")), name="fourier_b",
    )(yv, yv, m2)
    return out.reshape(bsz, seq, D_FOURIER)


def _dft_tables(seq):
    n1 = DFT_N1
    n2 = seq // n1
    c = np.arange(FOURIER_GROUP)
    ang = 2.0 * np.pi * np.outer(c, c) / FOURIER_GROUP
    eye = np.eye(D_FOURIER // FOURIER_GROUP)
    cc = np.kron(eye, np.cos(ang))
    sc = np.kron(eye, np.sin(ang))
    i1 = np.arange(n1)
    a1 = 2.0 * np.pi * np.outer(i1, i1) / n1
    c1, s1 = np.cos(a1), np.sin(a1)
    m1a = np.concatenate([c1, -s1], axis=0)
    m1b = np.concatenate([-s1, -c1], axis=0)
    k = i1[:, None] + n1 * np.arange(n2)[None, :]
    phi = 2.0 * np.pi * (k[:, :, None] * np.arange(n2)[None, None, :] % seq) / seq
    norm = 1.0 / math.sqrt(seq * FOURIER_GROUP)
    m2 = np.concatenate([np.cos(phi), np.sin(phi)], axis=-1) * norm
    bf = lambda t: jnp.asarray(t, jnp.float32).astype(jnp.bfloat16)
    return bf(cc), bf(sc), bf(m1a), bf(m1b), bf(m2)


def _outproj_kernel(o_ref, u_ref, up_ref, un_ref, b_ref, f_ref, h_ref, wom_ref, woc_ref, wof_ref, wconv_ref,
                    g_ref, beta_ref, wrh_ref, wrl_ref, br_ref, h1_ref, comb_ref, *, alpha):
    i = pl.program_id(1)
    last = pl.num_programs(1) - 1
    tm = u_ref.shape[1]
    u = u_ref[0]
    up = up_ref[0][7:8, :] * (i > 0).astype(jnp.float32)
    un = un_ref[0][0:1, :] * (i < last).astype(jnp.float32)
    row = lax.broadcasted_iota(jnp.int32, u.shape, 0)
    u_m1 = jnp.where(row == 0, up, pltpu.roll(u, 1, axis=0))
    u_p1 = jnp.where(row == tm - 1, un, pltpu.roll(u, tm - 1, axis=0))
    wc = wconv_ref[...]
    oconv = b_ref[0] * (u_m1 * wc[0:1, :] + u * wc[1:2, :] + u_p1 * wc[2:3, :])
    mix = _bdot(oconv.astype(jnp.bfloat16), woc_ref[...]) + _bdot(f_ref[0].astype(jnp.bfloat16), wof_ref[...])
    mix = mix + lax.dot_general(o_ref[0], wom_ref[...], (((0,), (0,)), ((), ())),
                                preferred_element_type=jnp.float32)
    h1 = _layer_norm(alpha * h_ref[0] + mix, g_ref[...], beta_ref[...])
    h1_ref[0] = h1

    h_hi = h1.astype(jnp.bfloat16)
    h_lo = (h1 - h_hi.astype(jnp.float32)).astype(jnp.bfloat16)
    logits = (_bdot(h_hi, wrh_ref[...]) + (_bdot(h_hi, wrl_ref[...]) + _bdot(h_lo, wrh_ref[...]))) + br_ref[...]
    lane = lax.broadcasted_iota(jnp.int32, logits.shape, 1)
    neg = jnp.float32(-jnp.inf)
    big = jnp.int32(1 << 20)
    is_g = lane < N_GROUPS
    gl = jnp.where(is_g, logits, neg)
    gmax = jnp.max(gl, axis=-1, keepdims=True)
    p_group = 1.0 / jnp.sum(jnp.exp(gl - gmax), axis=-1, keepdims=True)
    g_sel = jnp.min(jnp.where(is_g & (logits == gmax), lane, big), axis=-1, keepdims=True)
    e_lo = N_GROUPS + g_sel * EXPERTS_PER_GROUP
    is_e = (lane >= e_lo) & (lane < e_lo + EXPERTS_PER_GROUP)
    l1 = jnp.max(jnp.where(is_e, logits, neg), axis=-1, keepdims=True)
    i1 = jnp.min(jnp.where(is_e & (logits == l1), lane, big), axis=-1, keepdims=True)
    is_e2 = is_e & (lane != i1)
    l2 = jnp.max(jnp.where(is_e2, logits, neg), axis=-1, keepdims=True)
    i2 = jnp.min(jnp.where(is_e2 & (logits == l2), lane, big), axis=-1, keepdims=True)
    r = jnp.exp(l2 - l1)
    w1 = 1.0 / (1.0 + r)
    w2 = r * w1
    comb_ref[0] = jnp.where(lane == i1, w1, jnp.where(lane == i2, w2, 0.0)) * p_group


def _outproj(o, u, bgate, ofour, h, wom, woc, wof, wconv, g, beta, wrh, wrl, br, *, alpha, tm):
    bsz, seq, d = h.shape
    nblk8 = seq // 8
    r8 = tm // 8
    const2 = lambda b, i: (0, 0)
    const3 = lambda b, i: (0, 0, 0)
    tok = lambda w: pl.BlockSpec((1, tm, w), lambda b, i: (b, i, 0))
    return pl.pallas_call(
        functools.partial(_outproj_kernel, alpha=alpha),
        grid=(bsz, seq // tm),
        in_specs=[
            pl.BlockSpec((1, N_HEADS * V_DIM, tm), lambda b, i: (b, 0, i)),
            tok(D_CONV),
            pl.BlockSpec((1, 8, D_CONV), lambda b, i: (b, jnp.maximum(i * r8 - 1, 0), 0)),
            pl.BlockSpec((1, 8, D_CONV), lambda b, i: (b, jnp.minimum((i + 1) * r8, nblk8 - 1), 0)),
            tok(D_CONV),
            tok(D_FOURIER),
            tok(d),
            pl.BlockSpec(wom.shape, const2),
            pl.BlockSpec(woc.shape, const2),
            pl.BlockSpec(wof.shape, const2),
            pl.BlockSpec(wconv.shape, const2),
            pl.BlockSpec((1, d), const2),
            pl.BlockSpec((1, d), const2),
            pl.BlockSpec(wrh.shape, const2),
            pl.BlockSpec(wrl.shape, const2),
            pl.BlockSpec((1, ROUTER_LANES), const2),
        ],
        out_specs=[tok(d), tok(ROUTER_LANES)],
        out_shape=[jax.ShapeDtypeStruct((bsz, seq, d), jnp.float32),
                   jax.ShapeDtypeStruct((bsz, seq, ROUTER_LANES), jnp.float32)],
        compiler_params=_cparams(("parallel", "parallel")), name="outproj",
    )(o, u, u, u, bgate, ofour, h, wom, woc, wof, wconv, g, beta, wrh, wrl, br)


def _moe_kernel(h_ref, comb_ref, wg_ref, wu_ref, wd_ref, g_ref, beta_ref, o_ref, xb_ref, *, alpha):
    e = pl.program_id(1)

    @pl.when(e == 0)
    def _():
        xb_ref[...] = h_ref[...].astype(xb_ref.dtype)
        o_ref[...] = jnp.zeros_like(o_ref)

    xb = xb_ref[...]
    a = _bdot(xb, wg_ref[0])
    up = _bdot(xb, wu_ref[0])
    lane = lax.broadcasted_iota(jnp.int32, comb_ref.shape, 1)
    cw = jnp.sum(jnp.where(lane == e + N_GROUPS, comb_ref[...], 0.0), axis=-1, keepdims=True)
    hid = a * (1.0 / (1.0 + jnp.exp(-a))) * up * cw
    o_ref[...] += _bdot(hid.astype(jnp.bfloat16), wd_ref[0])

    @pl.when(e == pl.num_programs(1) - 1)
    def _():
        o_ref[...] = _layer_norm(alpha * h_ref[...] + o_ref[...], g_ref[...], beta_ref[...])


def _moe(h, comb, wg, wu, wd, g, beta, *, alpha, tm):
    n, d = h.shape
    const2 = lambda i, e: (0, 0)
    return pl.pallas_call(
        functools.partial(_moe_kernel, alpha=alpha),
        grid=(n // tm, N_EXPERTS),
        in_specs=[
            pl.BlockSpec((tm, d), lambda i, e: (i, 0)),
            pl.BlockSpec((tm, ROUTER_LANES), lambda i, e: (i, 0)),
            pl.BlockSpec((1, d, D_EXPERT), lambda i, e: (e, 0, 0)),
            pl.BlockSpec((1, d, D_EXPERT), lambda i, e: (e, 0, 0)),
            pl.BlockSpec((1, D_EXPERT, d), lambda i, e: (e, 0, 0)),
            pl.BlockSpec((1, d), const2),
            pl.BlockSpec((1, d), const2),
        ],
        out_specs=pl.BlockSpec((tm, d), lambda i, e: (i, 0)),
        out_shape=jax.ShapeDtypeStruct((n, d), jnp.float32),
        scratch_shapes=[pltpu.VMEM((tm, d), jnp.bfloat16)],
        compiler_params=_cparams(("parallel", "arbitrary")), name="moe",
    )(h, comb, wg, wu, wd, g, beta)


def _rope_tables(seq):
    inv_freq = 1.0 / (ROPE_THETA ** (jnp.arange(0, QK_ROPE, 2, dtype=jnp.float32) / QK_ROPE))
    ang = jnp.arange(seq, dtype=jnp.float32)[:, None] * inv_freq[None, :]
    cos, sin = jnp.cos(ang), jnp.sin(ang)
    c96 = jnp.concatenate([jnp.ones((seq, QK_NOPE), jnp.float32), cos, cos], axis=-1)
    s96 = jnp.concatenate([jnp.zeros((seq, QK_NOPE), jnp.float32), -sin, sin], axis=-1)
    return _pad_cols(c96, 0, K_LANES), _pad_cols(s96, 0, K_LANES), c96.T, s96.T


def _swap_halves(w):
    half = w.shape[-1] // 2
    return jnp.concatenate([w[..., half:], w[..., :half]], axis=-1)


def _pad_cols(w, left, total):
    return jnp.pad(w, [(0, 0)] * (w.ndim - 1) + [(left, total - left - w.shape[-1])])


def _pick_tile(n, want):
    t = min(n, want)
    while n % t:
        t //= 2
    return t


def kernel(x, ln_in_g, ln_in_b, w_in, g_q, g_kv, w_uq, w_ukv, w_conv, w_out, ln1_g, ln1_b, w_group, b_group,
           w_router, b_router, w_gate, w_up, w_down, ln2_g, ln2_b):
    bsz, seq, d = x.shape
    depth = w_in.shape[0]
    alpha = (2.0 * depth) ** 0.25
    bf16 = jnp.bfloat16
    tm = _pick_tile(seq, 512)
    tq = _pick_tile(seq, 1024)
    tk = tm
    tm_moe = _pick_tile(bsz * seq, 2048)

    c96, s96, c96t, s96t = _rope_tables(seq)
    cc, sc, m1a, m1b, m2 = _dft_tables(seq)
    row = lambda v: v.reshape(1, -1)
    d_mla = N_HEADS * V_DIM

    h = x
    for l in range(depth):
        wi = w_in[l]
        splits = np.cumsum([Q_LORA, KV_LORA, QK_ROPE, D_CONV, D_CONV, D_CONV])
        w_cq, w_ckv, w_kpe, w_b, w_c, w_h, w_f = jnp.split(wi, splits, axis=-1)
        win = jnp.concatenate([
            w_cq, w_ckv, w_b, w_c, w_h, w_f,
            _pad_cols(w_kpe, QK_NOPE, 128), _pad_cols(_swap_halves(w_kpe), QK_NOPE, 128)], axis=-1).astype(bf16)
        wq = jnp.transpose(w_uq[l], (1, 0, 2))
        wqs = _pad_cols(_swap_halves(wq[..., QK_NOPE:]), QK_NOPE, D_QK)
        wqt = jnp.transpose(wq, (0, 2, 1)).reshape(N_HEADS * D_QK, Q_LORA)
        wqst = jnp.transpose(wqs, (0, 2, 1)).reshape(N_HEADS * D_QK, Q_LORA)
        wkv = jnp.transpose(w_ukv[l], (1, 0, 2))
        wk = _pad_cols(wkv[..., :QK_NOPE], 0, K_LANES)
        wvt = _pad_cols(wkv[..., QK_NOPE:], 0, V_ROWS)
        wvt = jnp.transpose(wvt, (0, 2, 1)).reshape(N_HEADS * V_ROWS, KV_LORA)
        outs = _inproj(h, row(ln_in_g), row(ln_in_b), win, row(g_q[l]), row(g_kv[l]), wqt.astype(bf16),
                       wqst.astype(bf16), wk.astype(bf16), wvt.astype(bf16), c96, s96, c96t, s96t, cc, sc,
                       ln_input=(l == 0), tm=tm)
        if l == 0:
            h, outs = outs[0], outs[1:]
        qt, k, vt, u, bgate, ab = outs
        o_mla = _attention(qt, k, vt, tq=tq, tk=tk)
        o_four = _fourier(ab, m1a, m1b, m2)

        wo = w_out[l].astype(bf16)
        wom = wo[:d_mla]
        woc = wo[d_mla:d_mla + D_CONV]
        wof = wo[d_mla + D_CONV:]
        wr = _pad_cols(jnp.concatenate([w_group[l], w_router[l].reshape(d, N_EXPERTS)], axis=-1), 0, ROUTER_LANES)
        br = _pad_cols(jnp.concatenate([b_group[l], b_router[l].reshape(N_EXPERTS)])[None, :], 0, ROUTER_LANES)
        wrh = wr.astype(bf16)
        wrl = (wr - wrh.astype(jnp.float32)).astype(bf16)
        h1, comb = _outproj(o_mla, u, bgate, o_four, h, wom, woc, wof, w_conv[l], row(ln1_g[l]), row(ln1_b[l]),
                            wrh, wrl, br, alpha=alpha, tm=tm)

        wg = w_gate[l].reshape(N_EXPERTS, d, D_EXPERT).astype(bf16)
        wu = w_up[l].reshape(N_EXPERTS, d, D_EXPERT).astype(bf16)
        wd = w_down[l].reshape(N_EXPERTS, D_EXPERT, d).astype(bf16)
        h = _moe(h1.reshape(bsz * seq, d), comb.reshape(bsz * seq, ROUTER_LANES), wg, wu, wd,
                 row(ln2_g[l]), row(ln2_b[l]), alpha=alpha, tm=tm_moe).reshape(bsz, seq, d)
    return h
```

```python
import functools
import math

import numpy as np
import jax
import jax.numpy as jnp
from jax import lax
from jax.experimental import pallas as pl
from jax.experimental.pallas import tpu as pltpu

N_HEADS = 8
QK_NOPE = 64
QK_ROPE = 32
D_QK = QK_NOPE + QK_ROPE
K_LANES = 128
V_DIM = 64
V_ROWS = 80
Q_LORA = 384
KV_LORA = 256
D_CONV = 256
D_FOURIER = 256
FOURIER_GROUP = 64
N_GROUPS = 4
EXPERTS_PER_GROUP = 8
N_EXPERTS = N_GROUPS * EXPERTS_PER_GROUP
D_EXPERT = 256
ROPE_THETA = 10000.0
LN_EPS = 1e-5
RMS_EPS = 1e-6
DFT_N1 = 128
ROUTER_LANES = 128
GID_LANE = N_GROUPS + N_EXPERTS
MOE_TILE = 1024
MOE_BLK = 256
MOE_SPLIT = 1
VMEM_LIMIT = 56 * 1024 * 1024

_O_CQ = 0
_O_CKV = _O_CQ + Q_LORA
_O_B = _O_CKV + KV_LORA
_O_C = _O_B + D_CONV
_O_H = _O_C + D_CONV
_O_F = _O_H + D_CONV
_O_KPE = _O_F + D_FOURIER
_O_KPES = _O_KPE + 128
_W_IN_COLS = _O_KPES + 128


def _cparams(sem):
    return pltpu.CompilerParams(dimension_semantics=sem, vmem_limit_bytes=VMEM_LIMIT)


def _layer_norm(x, g, b):
    mu = jnp.mean(x, axis=-1, keepdims=True)
    xc = x - mu
    var = jnp.mean(xc * xc, axis=-1, keepdims=True)
    return xc * lax.rsqrt(var + LN_EPS) * g + b


def _rms_norm(x, g):
    return x * lax.rsqrt(jnp.mean(x * x, axis=-1, keepdims=True) + RMS_EPS) * g


def _bdot(a, b):
    return jnp.dot(a, b, preferred_element_type=jnp.float32)


def _dot_nt(a, b):
    return lax.dot_general(a, b, (((1,), (1,)), ((), ())), preferred_element_type=jnp.float32)


def _inproj_kernel(x_ref, lng_ref, lnb_ref, win_ref, gq_ref, gkv_ref, wqt_ref, wqst_ref, wk_ref, wvt_ref,
                   c96_ref, s96_ref, c96t_ref, s96t_ref, cc_ref, sc_ref, *out_refs, ln_input, q_scale):
    if ln_input:
        h_ref, qt_ref, k_ref, vt_ref, u_ref, b_ref, ab_ref = out_refs
        h = _layer_norm(x_ref[0], lng_ref[...], lnb_ref[...])
        h_ref[0] = h
    else:
        qt_ref, k_ref, vt_ref, u_ref, b_ref, ab_ref = out_refs
        h = x_ref[0]
    z = _bdot(h.astype(jnp.bfloat16), win_ref[...])
    c96 = c96_ref[...]
    s96 = s96_ref[...]
    c96t = c96t_ref[...]
    s96t = s96t_ref[...]
    cqn = _rms_norm(z[:, _O_CQ:_O_CQ + Q_LORA], gq_ref[...]).astype(jnp.bfloat16)
    ckvn = _rms_norm(z[:, _O_CKV:_O_CKV + KV_LORA], gkv_ref[...]).astype(jnp.bfloat16)
    kpe = z[:, _O_KPE:_O_KPE + K_LANES] * c96 + z[:, _O_KPES:_O_KPES + K_LANES] * s96
    qt_all = _dot_nt(wqt_ref[...], cqn)
    qst_all = _dot_nt(wqst_ref[...], cqn)
    vt_all = _dot_nt(wvt_ref[...], ckvn)
    vrow = lax.broadcasted_iota(jnp.int32, (V_ROWS, vt_all.shape[1]), 0)
    for hd in range(N_HEADS):
        qt = (qt_all[hd * D_QK:(hd + 1) * D_QK] * c96t + qst_all[hd * D_QK:(hd + 1) * D_QK] * s96t) * q_scale
        qt_ref[0, hd] = qt.astype(qt_ref.dtype)
        k_ref[0, hd] = (_bdot(ckvn, wk_ref[hd]) + kpe).astype(k_ref.dtype)
        vt = jnp.where(vrow == V_DIM, 1.0, vt_all[hd * V_ROWS:(hd + 1) * V_ROWS])
        vt_ref[0, hd, 0] = vt.astype(vt_ref.dtype)
    u_ref[0] = z[:, _O_C:_O_C + D_CONV] * z[:, _O_H:_O_H + D_CONV]
    b_ref[0] = z[:, _O_B:_O_B + D_CONV]
    fb = z[:, _O_F:_O_F + D_FOURIER].astype(jnp.bfloat16)
    ab_ref[0, :, :D_FOURIER] = _bdot(fb, cc_ref[...]).astype(ab_ref.dtype)
    ab_ref[0, :, D_FOURIER:] = _bdot(fb, sc_ref[...]).astype(ab_ref.dtype)


def _inproj(x, lng, lnb, win, gq, gkv, wqt, wqst, wk, wvt, c96, s96, c96t, s96t, cc, sc, *, ln_input, tm):
    bsz, seq, d = x.shape
    q_scale = (D_QK ** -0.5) * math.log2(math.e)
    grid = (bsz, seq // tm)
    const2 = lambda b, i: (0, 0)
    const3 = lambda b, i: (0, 0, 0)
    in_specs = [
        pl.BlockSpec((1, tm, d), lambda b, i: (b, i, 0)),
        pl.BlockSpec((1, d), const2),
        pl.BlockSpec((1, d), const2),
        pl.BlockSpec(win.shape, const2),
        pl.BlockSpec((1, Q_LORA), const2),
        pl.BlockSpec((1, KV_LORA), const2),
        pl.BlockSpec(wqt.shape, const2),
        pl.BlockSpec(wqst.shape, const2),
        pl.BlockSpec(wk.shape, const3),
        pl.BlockSpec(wvt.shape, const2),
        pl.BlockSpec((tm, K_LANES), lambda b, i: (i, 0)),
        pl.BlockSpec((tm, K_LANES), lambda b, i: (i, 0)),
        pl.BlockSpec((D_QK, tm), lambda b, i: (0, i)),
        pl.BlockSpec((D_QK, tm), lambda b, i: (0, i)),
        pl.BlockSpec(cc.shape, const2),
        pl.BlockSpec(sc.shape, const2),
    ]
    tok_spec = lambda w: pl.BlockSpec((1, tm, w), lambda b, i: (b, i, 0))
    out_shape = [
        jax.ShapeDtypeStruct((bsz, N_HEADS, D_QK, seq), jnp.bfloat16),
        jax.ShapeDtypeStruct((bsz, N_HEADS, seq, K_LANES), jnp.bfloat16),
        jax.ShapeDtypeStruct((bsz, N_HEADS, seq // tm, V_ROWS, tm), jnp.bfloat16),
        jax.ShapeDtypeStruct((bsz, seq, D_CONV), jnp.float32),
        jax.ShapeDtypeStruct((bsz, seq, D_CONV), jnp.float32),
        jax.ShapeDtypeStruct((bsz, seq, 2 * D_FOURIER), jnp.bfloat16),
    ]
    out_specs = [pl.BlockSpec((1, N_HEADS, D_QK, tm), lambda b, i: (b, 0, 0, i)),
                 pl.BlockSpec((1, N_HEADS, tm, K_LANES), lambda b, i: (b, 0, i, 0)),
                 pl.BlockSpec((1, N_HEADS, 1, V_ROWS, tm), lambda b, i: (b, 0, i, 0, 0)),
                 tok_spec(D_CONV), tok_spec(D_CONV), tok_spec(2 * D_FOURIER)]
    if ln_input:
        out_shape = [jax.ShapeDtypeStruct((bsz, seq, d), jnp.float32)] + out_shape
        out_specs = [tok_spec(d)] + out_specs
    return pl.pallas_call(
        functools.partial(_inproj_kernel, ln_input=ln_input, q_scale=q_scale),
        grid=grid, in_specs=in_specs, out_specs=out_specs, out_shape=out_shape,
        compiler_params=_cparams(("parallel", "parallel")), name="inproj",
    )(x, lng, lnb, win, gq, gkv, wqt, wqst, wk, wvt, c96, s96, c96t, s96t, cc, sc)


def _attn_kernel(qt_ref, k_ref, vt_ref, o_ref, s0_ref, s1_ref, m_ref, acc_ref, *, tk, n_kv):
    qt = qt_ref[0, 0]

    def scores(j, s_ref):
        off = pl.multiple_of(j * tk, tk)
        s_ref[...] = _bdot(k_ref[0, 0, pl.ds(off, tk), :D_QK], qt)

    def update(j, s_ref):
        s = s_ref[...]
        m_old = m_ref[...]
        m_new = jnp.maximum(m_old, jnp.max(s, axis=0, keepdims=True))
        p = jnp.exp2(s - m_new).astype(jnp.bfloat16)
        acc_ref[...] = jnp.exp2(m_old - m_new) * acc_ref[...] + _bdot(vt_ref[0, 0, j], p)
        m_ref[...] = m_new

    m_ref[...] = jnp.full(m_ref.shape, -jnp.inf, jnp.float32)
    acc_ref[...] = jnp.zeros(acc_ref.shape, jnp.float32)
    scores(0, s0_ref)

    def pair(i, carry):
        j = 2 * i
        scores(j + 1, s1_ref)
        update(j, s0_ref)
        scores(j + 2, s0_ref)
        update(j + 1, s1_ref)
        return carry

    lax.fori_loop(0, n_kv // 2 - 1, pair, 0)
    scores(n_kv - 1, s1_ref)
    update(n_kv - 2, s0_ref)
    update(n_kv - 1, s1_ref)

    acc = acc_ref[...]
    o_ref[0] = (acc[:V_DIM] / acc[V_DIM:V_DIM + 1]).astype(o_ref.dtype)


def _attention(qt, k, vt, *, tq, tk):
    bsz, nh, seq, _ = k.shape
    n_kv = seq // tk
    assert n_kv % 2 == 0 and vt.shape[2:] == (n_kv, V_ROWS, tk)
    return pl.pallas_call(
        functools.partial(_attn_kernel, tk=tk, n_kv=n_kv),
        grid=(bsz, nh, seq // tq),
        in_specs=[
            pl.BlockSpec((1, 1, D_QK, tq), lambda b, h, i: (b, h, 0, i)),
            pl.BlockSpec((1, 1, seq, K_LANES), lambda b, h, i: (b, h, 0, 0)),
            pl.BlockSpec((1, 1, n_kv, V_ROWS, tk), lambda b, h, i: (b, h, 0, 0, 0)),
        ],
        out_specs=pl.BlockSpec((1, V_DIM, tq), lambda b, h, i: (b, h, i)),
        out_shape=jax.ShapeDtypeStruct((bsz, nh * V_DIM, seq), jnp.bfloat16),
        scratch_shapes=[pltpu.VMEM((tk, tq), jnp.float32), pltpu.VMEM((tk, tq), jnp.float32),
                        pltpu.VMEM((1, tq), jnp.float32), pltpu.VMEM((V_ROWS, tq), jnp.float32)],
        compiler_params=_cparams(("parallel", "parallel", "arbitrary")), name="attention",
    )(qt, k, vt)


def _fourier_a_kernel(ab_ref, m1a_ref, m1b_ref, y_ref, *, n2c):
    for j in range(n2c):
        a = ab_ref[0, :, j * 2 * D_FOURIER: j * 2 * D_FOURIER + D_FOURIER]
        b = ab_ref[0, :, j * 2 * D_FOURIER + D_FOURIER: (j + 1) * 2 * D_FOURIER]
        y = _bdot(m1a_ref[...], a) + _bdot(m1b_ref[...], b)
        y_ref[0, :, j * D_FOURIER:(j + 1) * D_FOURIER] = y.astype(y_ref.dtype)


def _fourier_b_kernel(yr_ref, yi_ref, m2_ref, o_ref, *, k1c, n2):
    for j in range(k1c):
        m2 = m2_ref[j]
        z = _bdot(m2[:, :n2], yr_ref[0, j]) + _bdot(m2[:, n2:], yi_ref[0, j])
        o_ref[0, :, j, :] = z.astype(o_ref.dtype)


def _fourier(ab, m1a, m1b, m2):
    bsz, seq, _ = ab.shape
    n1 = DFT_N1
    n2 = seq // n1
    n2c = min(n2, 16)
    k1c = 8
    abv = ab.reshape(bsz, n1, n2 * 2 * D_FOURIER)
    y = pl.pallas_call(
        functools.partial(_fourier_a_kernel, n2c=n2c),
        grid=(bsz, n2 // n2c),
        in_specs=[
            pl.BlockSpec((1, n1, n2c * 2 * D_FOURIER), lambda b, c: (b, 0, c)),
            pl.BlockSpec(m1a.shape, lambda b, c: (0, 0)),
            pl.BlockSpec(m1b.shape, lambda b, c: (0, 0)),
        ],
        out_specs=pl.BlockSpec((1, 2 * n1, n2c * D_FOURIER), lambda b, c: (b, 0, c)),
        out_shape=jax.ShapeDtypeStruct((bsz, 2 * n1, n2 * D_FOURIER), jnp.bfloat16),
        compiler_params=_cparams(("parallel", "parallel")), name="fourier_a",
    )(abv, m1a, m1b)
    yv = y.reshape(bsz, 2 * n1, n2, D_FOURIER)
    nblk = n1 // k1c
    out = pl.pallas_call(
        functools.partial(_fourier_b_kernel, k1c=k1c, n2=n2),
        grid=(bsz, nblk),
        in_specs=[
            pl.BlockSpec((1, k1c, n2, D_FOURIER), lambda b, c: (b, c, 0, 0)),
            pl.BlockSpec((1, k1c, n2, D_FOURIER), lambda b, c: (b, nblk + c, 0, 0)),
            pl.BlockSpec((k1c, n2, 2 * n2), lambda b, c: (c, 0, 0)),
        ],
        out_specs=pl.BlockSpec((1, n2, k1c, D_FOURIER), lambda b, c: (b, 0, c, 0)),
        out_shape=jax.ShapeDtypeStruct((bsz, n2, n1, D_FOURIER), jnp.float32),
        compiler_params=_cparams(("parallel", "parallel")), name="fourier_b",
    )(yv, yv, m2)
    return out.reshape(bsz, seq, D_FOURIER)


def _dft_tables(seq):
    n1 = DFT_N1
    n2 = seq // n1
    c = np.arange(FOURIER_GROUP)
    ang = 2.0 * np.pi * np.outer(c, c) / FOURIER_GROUP
    eye = np.eye(D_FOURIER // FOURIER_GROUP)
    cc = np.kron(eye, np.cos(ang))
    sc = np.kron(eye, np.sin(ang))
    i1 = np.arange(n1)
    a1 = 2.0 * np.pi * np.outer(i1, i1) / n1
    c1, s1 = np.cos(a1), np.sin(a1)
    m1a = np.concatenate([c1, -s1], axis=0)
    m1b = np.concatenate([-s1, -c1], axis=0)
    k = i1[:, None] + n1 * np.arange(n2)[None, :]
    phi = 2.0 * np.pi * (k[:, :, None] * np.arange(n2)[None, None, :] % seq) / seq
    norm = 1.0 / math.sqrt(seq * FOURIER_GROUP)
    m2 = np.concatenate([np.cos(phi), np.sin(phi)], axis=-1) * norm
    bf = lambda t: jnp.asarray(t, jnp.float32).astype(jnp.bfloat16)
    return bf(cc), bf(sc), bf(m1a), bf(m1b), bf(m2)


def _outproj_kernel(o_ref, u_ref, up_ref, un_ref, b_ref, f_ref, h_ref, wom_ref, woc_ref, wof_ref, wconv_ref,
                    g_ref, beta_ref, wrh_ref, wrl_ref, br_ref, h1_ref, comb_ref, cnt_ref, *, alpha):
    i = pl.program_id(1)
    last = pl.num_programs(1) - 1
    tm = u_ref.shape[1]
    u = u_ref[0]
    up = up_ref[0][7:8, :] * (i > 0).astype(jnp.float32)
    un = un_ref[0][0:1, :] * (i < last).astype(jnp.float32)
    row = lax.broadcasted_iota(jnp.int32, u.shape, 0)
    u_m1 = jnp.where(row == 0, up, pltpu.roll(u, 1, axis=0))
    u_p1 = jnp.where(row == tm - 1, un, pltpu.roll(u, tm - 1, axis=0))
    wc = wconv_ref[...]
    oconv = b_ref[0] * (u_m1 * wc[0:1, :] + u * wc[1:2, :] + u_p1 * wc[2:3, :])
    mix = _bdot(oconv.astype(jnp.bfloat16), woc_ref[...]) + _bdot(f_ref[0].astype(jnp.bfloat16), wof_ref[...])
    mix = mix + lax.dot_general(o_ref[0], wom_ref[...], (((0,), (0,)), ((), ())),
                                preferred_element_type=jnp.float32)
    h1 = _layer_norm(alpha * h_ref[0] + mix, g_ref[...], beta_ref[...])
    h1_ref[0] = h1

    h_hi = h1.astype(jnp.bfloat16)
    h_lo = (h1 - h_hi.astype(jnp.float32)).astype(jnp.bfloat16)
    logits = (_bdot(h_hi, wrh_ref[...]) + (_bdot(h_hi, wrl_ref[...]) + _bdot(h_lo, wrh_ref[...]))) + br_ref[...]
    lane = lax.broadcasted_iota(jnp.int32, logits.shape, 1)
    neg = jnp.float32(-jnp.inf)
    big = jnp.int32(1 << 20)
    is_g = lane < N_GROUPS
    gl = jnp.where(is_g, logits, neg)
    gmax = jnp.max(gl, axis=-1, keepdims=True)
    p_group = 1.0 / jnp.sum(jnp.exp(gl - gmax), axis=-1, keepdims=True)
    g_sel = jnp.min(jnp.where(is_g & (logits == gmax), lane, big), axis=-1, keepdims=True)
    e_lo = N_GROUPS + g_sel * EXPERTS_PER_GROUP
    is_e = (lane >= e_lo) & (lane < e_lo + EXPERTS_PER_GROUP)
    l1 = jnp.max(jnp.where(is_e, logits, neg), axis=-1, keepdims=True)
    i1 = jnp.min(jnp.where(is_e & (logits == l1), lane, big), axis=-1, keepdims=True)
    is_e2 = is_e & (lane != i1)
    l2 = jnp.max(jnp.where(is_e2, logits, neg), axis=-1, keepdims=True)
    i2 = jnp.min(jnp.where(is_e2 & (logits == l2), lane, big), axis=-1, keepdims=True)
    r = jnp.exp(l2 - l1)
    w1 = 1.0 / (1.0 + r)
    w2 = r * w1
    comb = jnp.where(lane == i1, w1, jnp.where(lane == i2, w2, 0.0)) * p_group
    comb_ref[0] = jnp.where(lane == GID_LANE, g_sel.astype(jnp.float32), comb)
    cnt_ref[0, 0] = jnp.sum(jnp.where(lane == g_sel, 1.0, 0.0), axis=0, keepdims=True)


def _outproj(o, u, bgate, ofour, h, wom, woc, wof, wconv, g, beta, wrh, wrl, br, *, alpha, tm):
    bsz, seq, d = h.shape
    nblk8 = seq // 8
    r8 = tm // 8
    const2 = lambda b, i: (0, 0)
    const3 = lambda b, i: (0, 0, 0)
    tok = lambda w: pl.BlockSpec((1, tm, w), lambda b, i: (b, i, 0))
    return pl.pallas_call(
        functools.partial(_outproj_kernel, alpha=alpha),
        grid=(bsz, seq // tm),
        in_specs=[
            pl.BlockSpec((1, N_HEADS * V_DIM, tm), lambda b, i: (b, 0, i)),
            tok(D_CONV),
            pl.BlockSpec((1, 8, D_CONV), lambda b, i: (b, jnp.maximum(i * r8 - 1, 0), 0)),
            pl.BlockSpec((1, 8, D_CONV), lambda b, i: (b, jnp.minimum((i + 1) * r8, nblk8 - 1), 0)),
            tok(D_CONV),
            tok(D_FOURIER),
            tok(d),
            pl.BlockSpec(wom.shape, const2),
            pl.BlockSpec(woc.shape, const2),
            pl.BlockSpec(wof.shape, const2),
            pl.BlockSpec(wconv.shape, const2),
            pl.BlockSpec((1, d), const2),
            pl.BlockSpec((1, d), const2),
            pl.BlockSpec(wrh.shape, const2),
            pl.BlockSpec(wrl.shape, const2),
            pl.BlockSpec((1, ROUTER_LANES), const2),
        ],
        out_specs=[tok(d), tok(ROUTER_LANES), pl.BlockSpec((1, 1, 1, ROUTER_LANES), lambda b, i: (b, i, 0, 0))],
        out_shape=[jax.ShapeDtypeStruct((bsz, seq, d), jnp.float32),
                   jax.ShapeDtypeStruct((bsz, seq, ROUTER_LANES), jnp.float32),
                   jax.ShapeDtypeStruct((bsz, seq // tm, 1, ROUTER_LANES), jnp.float32)],
        compiler_params=_cparams(("parallel", "parallel")), name="outproj",
    )(o, u, u, u, bgate, ofour, h, wom, woc, wof, wconv, g, beta, wrh, wrl, br)


def _dot_tn(a, b):
    return lax.dot_general(a, b, (((0,), (0,)), ((), ())), preferred_element_type=jnp.float32)


def _moe_kernel(offs_ref, h_ref, comb_ref, wgu_ref, wd_ref, g_ref, beta_ref, o_ref, pt_ref, xs_ref, cw_ref, ys_ref,
                *, alpha):
    i = pl.program_id(0)
    s = pl.program_id(1)
    grp = s // MOE_SPLIT
    n_blk = MOE_TILE // MOE_BLK
    e_step = EXPERTS_PER_GROUP // MOE_SPLIT
    f32, bf16 = jnp.float32, jnp.bfloat16

    @pl.when(s == 0)
    def _():
        comb = comb_ref[...]
        lane = lax.broadcasted_iota(jnp.int32, comb.shape, 1)
        in_grp = lane.astype(f32) == comb[:, GID_LANE:GID_LANE + 1]
        row = lax.broadcasted_iota(jnp.int32, (MOE_TILE, MOE_TILE), 0)
        col = lax.broadcasted_iota(jnp.int32, (MOE_TILE, MOE_TILE), 1)
        lower = jnp.where(row >= col, 1.0, 0.0).astype(bf16)
        rank = _bdot(lower, jnp.where(in_grp, 1.0, 0.0).astype(bf16))
        lane1 = lax.broadcasted_iota(jnp.int32, (1, ROUTER_LANES), 1)
        start = jnp.zeros((1, ROUTER_LANES), f32)
        for gg in range(N_GROUPS):
            start = jnp.where(lane1 == gg, offs_ref[i, gg].astype(f32), start)
        pos = jnp.sum(jnp.where(in_grp, start + rank - 1.0, 0.0), axis=-1, keepdims=True)
        pt = jnp.where(col.astype(f32) == pos, 1.0, 0.0).astype(bf16)
        pt_ref[...] = pt
        xs = _dot_tn(h_ref[...].astype(bf16), pt)
        c1 = comb.astype(bf16)
        r1 = comb - c1.astype(f32)
        c2 = r1.astype(bf16)
        c3 = (r1 - c2.astype(f32)).astype(bf16)
        cw = _dot_tn(c1, pt) + _dot_tn(c2, pt) + _dot_tn(c3, pt)
        for bb in range(n_blk):
            xs_ref[bb] = xs[:, bb * MOE_BLK:(bb + 1) * MOE_BLK].astype(bf16)
            cw_ref[bb] = cw[:, bb * MOE_BLK:(bb + 1) * MOE_BLK]
        ys_ref[...] = jnp.zeros_like(ys_ref)

    lo = offs_ref[i, grp]
    hi = offs_ref[i, grp + 1]
    b_lo = lo // MOE_BLK
    b_hi = jnp.where(hi > lo, (hi + MOE_BLK - 1) // MOE_BLK, b_lo)
    cw_row = N_GROUPS + s * e_step

    def block(b, carry):
        au = _bdot(wgu_ref[0], xs_ref[b])
        hid = []
        for e in range(e_step):
            a = au[2 * e * D_EXPERT:(2 * e + 1) * D_EXPERT]
            up = au[(2 * e + 1) * D_EXPERT:(2 * e + 2) * D_EXPERT]
            w = cw_ref[b, pl.ds(cw_row + e, 1), :]
            hid.append((a * (1.0 / (1.0 + jnp.exp(-a))) * up * w).astype(bf16))
        ys_ref[b] += _bdot(wd_ref[0], jnp.concatenate(hid, axis=0))
        return carry

    lax.fori_loop(b_lo, b_hi, block, 0)

    @pl.when(s == pl.num_programs(1) - 1)
    def _():
        ys = jnp.concatenate([ys_ref[bb] for bb in range(n_blk)], axis=1)
        y_hi = ys.astype(bf16)
        y_lo = (ys - y_hi.astype(f32)).astype(bf16)
        pt = pt_ref[...]
        y = _dot_nt(pt, y_hi) + _dot_nt(pt, y_lo)
        o_ref[...] = _layer_norm(alpha * h_ref[...] + y, g_ref[...], beta_ref[...])


def _moe(h, comb, offs, wgu, wd, g, beta, *, alpha):
    n, d = h.shape
    n_steps = N_GROUPS * MOE_SPLIT
    const2 = lambda i, s, offs: (0, 0)
    grid_spec = pltpu.PrefetchScalarGridSpec(
        num_scalar_prefetch=1,
        grid=(n // MOE_TILE, n_steps),
        in_specs=[
            pl.BlockSpec((MOE_TILE, d), lambda i, s, offs: (i, 0), pipeline_mode=pl.Buffered(1)),
            pl.BlockSpec((MOE_TILE, ROUTER_LANES), lambda i, s, offs: (i, 0), pipeline_mode=pl.Buffered(1)),
            pl.BlockSpec((1,) + wgu.shape[1:], lambda i, s, offs: (s, 0, 0)),
            pl.BlockSpec((1,) + wd.shape[1:], lambda i, s, offs: (s, 0, 0)),
            pl.BlockSpec((1, d), const2),
            pl.BlockSpec((1, d), const2),
        ],
        out_specs=pl.BlockSpec((MOE_TILE, d), lambda i, s, offs: (i, 0)),
        scratch_shapes=[
            pltpu.VMEM((MOE_TILE, MOE_TILE), jnp.bfloat16),
            pltpu.VMEM((MOE_TILE // MOE_BLK, d, MOE_BLK), jnp.bfloat16),
            pltpu.VMEM((MOE_TILE // MOE_BLK, ROUTER_LANES, MOE_BLK), jnp.float32),
            pltpu.VMEM((MOE_TILE // MOE_BLK, d, MOE_BLK), jnp.float32),
        ],
    )
    return pl.pallas_call(
        functools.partial(_moe_kernel, alpha=alpha),
        grid_spec=grid_spec,
        out_shape=jax.ShapeDtypeStruct((n, d), jnp.float32),
        compiler_params=_cparams(("parallel", "arbitrary")), name="moe",
    )(offs, h, comb, wgu, wd, g, beta)


def _moe_weights(w_gate, w_up, w_down):
    d = w_gate.shape[-2]
    n_steps = N_GROUPS * MOE_SPLIT
    e_step = EXPERTS_PER_GROUP // MOE_SPLIT
    wg = jnp.swapaxes(w_gate.reshape(N_EXPERTS, d, D_EXPERT), 1, 2)
    wu = jnp.swapaxes(w_up.reshape(N_EXPERTS, d, D_EXPERT), 1, 2)
    wgu = jnp.stack([wg, wu], axis=1).reshape(n_steps, e_step * 2 * D_EXPERT, d)
    wd = w_down.reshape(n_steps, e_step, D_EXPERT, d)
    wd = jnp.transpose(wd, (0, 3, 1, 2)).reshape(n_steps, d, e_step * D_EXPERT)
    return wgu.astype(jnp.bfloat16), wd.astype(jnp.bfloat16)


def _group_offsets(cnt, tiles_per_moe_tile):
    c = cnt.reshape(-1, tiles_per_moe_tile, ROUTER_LANES)[:, :, :N_GROUPS].sum(axis=1)
    ends = jnp.cumsum(c, axis=-1)
    offs = jnp.concatenate([jnp.zeros_like(ends[:, :1]), ends], axis=-1)
    return jnp.pad(offs, ((0, 0), (0, 8 - offs.shape[1]))).astype(jnp.int32)


def _rope_tables(seq):
    inv_freq = 1.0 / (ROPE_THETA ** (jnp.arange(0, QK_ROPE, 2, dtype=jnp.float32) / QK_ROPE))
    ang = jnp.arange(seq, dtype=jnp.float32)[:, None] * inv_freq[None, :]
    cos, sin = jnp.cos(ang), jnp.sin(ang)
    c96 = jnp.concatenate([jnp.ones((seq, QK_NOPE), jnp.float32), cos, cos], axis=-1)
    s96 = jnp.concatenate([jnp.zeros((seq, QK_NOPE), jnp.float32), -sin, sin], axis=-1)
    return _pad_cols(c96, 0, K_LANES), _pad_cols(s96, 0, K_LANES), c96.T, s96.T


def _swap_halves(w):
    half = w.shape[-1] // 2
    return jnp.concatenate([w[..., half:], w[..., :half]], axis=-1)


def _pad_cols(w, left, total):
    return jnp.pad(w, [(0, 0)] * (w.ndim - 1) + [(left, total - left - w.shape[-1])])


def _pick_tile(n, want):
    t = min(n, want)
    while n % t:
        t //= 2
    return t


def kernel(x, ln_in_g, ln_in_b, w_in, g_q, g_kv, w_uq, w_ukv, w_conv, w_out, ln1_g, ln1_b, w_group, b_group,
           w_router, b_router, w_gate, w_up, w_down, ln2_g, ln2_b):
    bsz, seq, d = x.shape
    depth = w_in.shape[0]
    alpha = (2.0 * depth) ** 0.25
    bf16 = jnp.bfloat16
    tm = _pick_tile(seq, 512)
    tq = _pick_tile(seq, 1024)
    tk = tm

    c96, s96, c96t, s96t = _rope_tables(seq)
    cc, sc, m1a, m1b, m2 = _dft_tables(seq)
    row = lambda v: v.reshape(1, -1)
    d_mla = N_HEADS * V_DIM

    h = x
    for l in range(depth):
        wi = w_in[l]
        splits = np.cumsum([Q_LORA, KV_LORA, QK_ROPE, D_CONV, D_CONV, D_CONV])
        w_cq, w_ckv, w_kpe, w_b, w_c, w_h, w_f = jnp.split(wi, splits, axis=-1)
        win = jnp.concatenate([
            w_cq, w_ckv, w_b, w_c, w_h, w_f,
            _pad_cols(w_kpe, QK_NOPE, 128), _pad_cols(_swap_halves(w_kpe), QK_NOPE, 128)], axis=-1).astype(bf16)
        wq = jnp.transpose(w_uq[l], (1, 0, 2))
        wqs = _pad_cols(_swap_halves(wq[..., QK_NOPE:]), QK_NOPE, D_QK)
        wqt = jnp.transpose(wq, (0, 2, 1)).reshape(N_HEADS * D_QK, Q_LORA)
        wqst = jnp.transpose(wqs, (0, 2, 1)).reshape(N_HEADS * D_QK, Q_LORA)
        wkv = jnp.transpose(w_ukv[l], (1, 0, 2))
        wk = _pad_cols(wkv[..., :QK_NOPE], 0, K_LANES)
        wvt = _pad_cols(wkv[..., QK_NOPE:], 0, V_ROWS)
        wvt = jnp.transpose(wvt, (0, 2, 1)).reshape(N_HEADS * V_ROWS, KV_LORA)
        outs = _inproj(h, row(ln_in_g), row(ln_in_b), win, row(g_q[l]), row(g_kv[l]), wqt.astype(bf16),
                       wqst.astype(bf16), wk.astype(bf16), wvt.astype(bf16), c96, s96, c96t, s96t, cc, sc,
                       ln_input=(l == 0), tm=tm)
        if l == 0:
            h, outs = outs[0], outs[1:]
        qt, k, vt, u, bgate, ab = outs
        o_mla = _attention(qt, k, vt, tq=tq, tk=tk)
        o_four = _fourier(ab, m1a, m1b, m2)

        wo = w_out[l].astype(bf16)
        wom = wo[:d_mla]
        woc = wo[d_mla:d_mla + D_CONV]
        wof = wo[d_mla + D_CONV:]
        wr = _pad_cols(jnp.concatenate([w_group[l], w_router[l].reshape(d, N_EXPERTS)], axis=-1), 0, ROUTER_LANES)
        br = _pad_cols(jnp.concatenate([b_group[l], b_router[l].reshape(N_EXPERTS)])[None, :], 0, ROUTER_LANES)
        wrh = wr.astype(bf16)
        wrl = (wr - wrh.astype(jnp.float32)).astype(bf16)
        h1, comb, cnt = _outproj(o_mla, u, bgate, o_four, h, wom, woc, wof, w_conv[l], row(ln1_g[l]), row(ln1_b[l]),
                            wrh, wrl, br, alpha=alpha, tm=tm)

        wgu, wd = _moe_weights(w_gate[l], w_up[l], w_down[l])
        offs = _group_offsets(cnt, MOE_TILE // tm)
        h = _moe(h1.reshape(bsz * seq, d), comb.reshape(bsz * seq, ROUTER_LANES), offs, wgu, wd,
                 row(ln2_g[l]), row(ln2_b[l]), alpha=alpha).reshape(bsz, seq, d)
    return h
```

```python
import functools
import math

import numpy as np
import jax
import jax.numpy as jnp
from jax import lax
from jax.experimental import pallas as pl
from jax.experimental.pallas import tpu as pltpu

N_HEADS = 8
QK_NOPE = 64
QK_ROPE = 32
D_QK = QK_NOPE + QK_ROPE
K_LANES = 128
V_DIM = 64
V_ROWS = 80
Q_LORA = 384
KV_LORA = 256
D_CONV = 256
D_FOURIER = 256
FOURIER_GROUP = 64
N_GROUPS = 4
EXPERTS_PER_GROUP = 8
N_EXPERTS = N_GROUPS * EXPERTS_PER_GROUP
D_EXPERT = 256
ROPE_THETA = 10000.0
LN_EPS = 1e-5
RMS_EPS = 1e-6
DFT_N1 = 128
ROUTER_LANES = 128
GID_LANE = N_GROUPS + N_EXPERTS
MOE_TILE = 1024
MOE_BLK = 256
MOE_SPLIT = 1
VMEM_LIMIT = 56 * 1024 * 1024

_O_CQ = 0
_O_CKV = _O_CQ + Q_LORA
_O_B = _O_CKV + KV_LORA
_O_C = _O_B + D_CONV
_O_H = _O_C + D_CONV
_O_F = _O_H + D_CONV
_O_KPE = _O_F + D_FOURIER
_O_KPES = _O_KPE + 128
_W_IN_COLS = _O_KPES + 128


def _cparams(sem):
    return pltpu.CompilerParams(dimension_semantics=sem, vmem_limit_bytes=VMEM_LIMIT)


def _layer_norm(x, g, b):
    mu = jnp.mean(x, axis=-1, keepdims=True)
    xc = x - mu
    var = jnp.mean(xc * xc, axis=-1, keepdims=True)
    return xc * lax.rsqrt(var + LN_EPS) * g + b


def _rms_norm(x, g):
    return x * lax.rsqrt(jnp.mean(x * x, axis=-1, keepdims=True) + RMS_EPS) * g


def _bdot(a, b):
    return jnp.dot(a, b, preferred_element_type=jnp.float32)


def _dot_nt(a, b):
    return lax.dot_general(a, b, (((1,), (1,)), ((), ())), preferred_element_type=jnp.float32)


def _inproj_kernel(x_ref, lng_ref, lnb_ref, win_ref, gq_ref, gkv_ref, wqt_ref, wqst_ref, wk_ref, wvt_ref,
                   c96_ref, s96_ref, c96t_ref, s96t_ref, cc_ref, sc_ref, *out_refs, ln_input, q_scale):
    if ln_input:
        h_ref, qt_ref, k_ref, vt_ref, u_ref, b_ref, ab_ref = out_refs
        h = _layer_norm(x_ref[0], lng_ref[...], lnb_ref[...])
        h_ref[0] = h
    else:
        qt_ref, k_ref, vt_ref, u_ref, b_ref, ab_ref = out_refs
        h = x_ref[0]
    z = _bdot(h.astype(jnp.bfloat16), win_ref[...])
    c96 = c96_ref[...]
    s96 = s96_ref[...]
    c96t = c96t_ref[...]
    s96t = s96t_ref[...]
    cqn = _rms_norm(z[:, _O_CQ:_O_CQ + Q_LORA], gq_ref[...]).astype(jnp.bfloat16)
    ckvn = _rms_norm(z[:, _O_CKV:_O_CKV + KV_LORA], gkv_ref[...]).astype(jnp.bfloat16)
    kpe = z[:, _O_KPE:_O_KPE + K_LANES] * c96 + z[:, _O_KPES:_O_KPES + K_LANES] * s96
    qt_all = _dot_nt(wqt_ref[...], cqn)
    qst_all = _dot_nt(wqst_ref[...], cqn)
    vt_all = _dot_nt(wvt_ref[...], ckvn)
    vrow = lax.broadcasted_iota(jnp.int32, (V_ROWS, vt_all.shape[1]), 0)
    for hd in range(N_HEADS):
        qt = (qt_all[hd * D_QK:(hd + 1) * D_QK] * c96t + qst_all[hd * D_QK:(hd + 1) * D_QK] * s96t) * q_scale
        qt_ref[0, hd] = qt.astype(qt_ref.dtype)
        k_ref[0, hd] = (_bdot(ckvn, wk_ref[hd]) + kpe).astype(k_ref.dtype)
        vt = jnp.where(vrow == V_DIM, 1.0, vt_all[hd * V_ROWS:(hd + 1) * V_ROWS])
        vt_ref[0, hd, 0] = vt.astype(vt_ref.dtype)
    u_ref[0] = z[:, _O_C:_O_C + D_CONV] * z[:, _O_H:_O_H + D_CONV]
    b_ref[0] = z[:, _O_B:_O_B + D_CONV]
    fb = z[:, _O_F:_O_F + D_FOURIER].astype(jnp.bfloat16)
    ab_ref[0, :, :D_FOURIER] = _bdot(fb, cc_ref[...]).astype(ab_ref.dtype)
    ab_ref[0, :, D_FOURIER:] = _bdot(fb, sc_ref[...]).astype(ab_ref.dtype)


def _inproj(x, lng, lnb, win, gq, gkv, wqt, wqst, wk, wvt, c96, s96, c96t, s96t, cc, sc, *, ln_input, tm):
    bsz, seq, d = x.shape
    q_scale = (D_QK ** -0.5) * math.log2(math.e)
    grid = (bsz, seq // tm)
    const2 = lambda b, i: (0, 0)
    const3 = lambda b, i: (0, 0, 0)
    in_specs = [
        pl.BlockSpec((1, tm, d), lambda b, i: (b, i, 0)),
        pl.BlockSpec((1, d), const2),
        pl.BlockSpec((1, d), const2),
        pl.BlockSpec(win.shape, const2),
        pl.BlockSpec((1, Q_LORA), const2),
        pl.BlockSpec((1, KV_LORA), const2),
        pl.BlockSpec(wqt.shape, const2),
        pl.BlockSpec(wqst.shape, const2),
        pl.BlockSpec(wk.shape, const3),
        pl.BlockSpec(wvt.shape, const2),
        pl.BlockSpec((tm, K_LANES), lambda b, i: (i, 0)),
        pl.BlockSpec((tm, K_LANES), lambda b, i: (i, 0)),
        pl.BlockSpec((D_QK, tm), lambda b, i: (0, i)),
        pl.BlockSpec((D_QK, tm), lambda b, i: (0, i)),
        pl.BlockSpec(cc.shape, const2),
        pl.BlockSpec(sc.shape, const2),
    ]
    tok_spec = lambda w: pl.BlockSpec((1, tm, w), lambda b, i: (b, i, 0))
    out_shape = [
        jax.ShapeDtypeStruct((bsz, N_HEADS, D_QK, seq), jnp.bfloat16),
        jax.ShapeDtypeStruct((bsz, N_HEADS, seq, K_LANES), jnp.bfloat16),
        jax.ShapeDtypeStruct((bsz, N_HEADS, seq // tm, V_ROWS, tm), jnp.bfloat16),
        jax.ShapeDtypeStruct((bsz, seq, D_CONV), jnp.float32),
        jax.ShapeDtypeStruct((bsz, seq, D_CONV), jnp.float32),
        jax.ShapeDtypeStruct((bsz, seq, 2 * D_FOURIER), jnp.bfloat16),
    ]
    out_specs = [pl.BlockSpec((1, N_HEADS, D_QK, tm), lambda b, i: (b, 0, 0, i)),
                 pl.BlockSpec((1, N_HEADS, tm, K_LANES), lambda b, i: (b, 0, i, 0)),
                 pl.BlockSpec((1, N_HEADS, 1, V_ROWS, tm), lambda b, i: (b, 0, i, 0, 0)),
                 tok_spec(D_CONV), tok_spec(D_CONV), tok_spec(2 * D_FOURIER)]
    if ln_input:
        out_shape = [jax.ShapeDtypeStruct((bsz, seq, d), jnp.float32)] + out_shape
        out_specs = [tok_spec(d)] + out_specs
    return pl.pallas_call(
        functools.partial(_inproj_kernel, ln_input=ln_input, q_scale=q_scale),
        grid=grid, in_specs=in_specs, out_specs=out_specs, out_shape=out_shape,
        compiler_params=_cparams(("parallel", "parallel")), name="inproj",
    )(x, lng, lnb, win, gq, gkv, wqt, wqst, wk, wvt, c96, s96, c96t, s96t, cc, sc)


def _attn_kernel(qt_ref, k_ref, vt_ref, o_ref, s0_ref, s1_ref, m_ref, acc_ref, *, tk, n_kv, span):
    qt = qt_ref[0, 0]
    n_span = n_kv // span

    def scores(g, s_ref):
        off = pl.multiple_of(g * (span * tk), span * tk)
        s_ref[...] = _bdot(k_ref[0, 0, pl.ds(off, span * tk), :D_QK], qt)

    def update(g, s_ref):
        for c in range(span):
            s = s_ref[c * tk:(c + 1) * tk, :]
            m_old = m_ref[...]
            m_new = jnp.maximum(m_old, jnp.max(s, axis=0, keepdims=True))
            p = jnp.exp2(s - m_new).astype(jnp.bfloat16)
            acc_ref[...] = jnp.exp2(m_old - m_new) * acc_ref[...] + _bdot(vt_ref[0, 0, g * span + c], p)
            m_ref[...] = m_new

    m_ref[...] = jnp.full(m_ref.shape, -jnp.inf, jnp.float32)
    acc_ref[...] = jnp.zeros(acc_ref.shape, jnp.float32)
    scores(0, s0_ref)

    def pair(i, carry):
        g = 2 * i
        scores(g + 1, s1_ref)
        update(g, s0_ref)
        scores(g + 2, s0_ref)
        update(g + 1, s1_ref)
        return carry

    lax.fori_loop(0, n_span // 2 - 1, pair, 0)
    scores(n_span - 1, s1_ref)
    update(n_span - 2, s0_ref)
    update(n_span - 1, s1_ref)

    acc = acc_ref[...]
    o_ref[0] = (acc[:V_DIM] / acc[V_DIM:V_DIM + 1]).astype(o_ref.dtype)


def _attention(qt, k, vt, *, tq, tk):
    bsz, nh, seq, _ = k.shape
    n_kv = seq // tk
    span = 2 if n_kv % 4 == 0 else 1
    assert n_kv % (2 * span) == 0 and vt.shape[2:] == (n_kv, V_ROWS, tk)
    return pl.pallas_call(
        functools.partial(_attn_kernel, tk=tk, n_kv=n_kv, span=span),
        grid=(bsz, nh, seq // tq),
        in_specs=[
            pl.BlockSpec((1, 1, D_QK, tq), lambda b, h, i: (b, h, 0, i)),
            pl.BlockSpec((1, 1, seq, K_LANES), lambda b, h, i: (b, h, 0, 0)),
            pl.BlockSpec((1, 1, n_kv, V_ROWS, tk), lambda b, h, i: (b, h, 0, 0, 0)),
        ],
        out_specs=pl.BlockSpec((1, V_DIM, tq), lambda b, h, i: (b, h, i)),
        out_shape=jax.ShapeDtypeStruct((bsz, nh * V_DIM, seq), jnp.bfloat16),
        scratch_shapes=[pltpu.VMEM((span * tk, tq), jnp.float32), pltpu.VMEM((span * tk, tq), jnp.float32),
                        pltpu.VMEM((1, tq), jnp.float32), pltpu.VMEM((V_ROWS, tq), jnp.float32)],
        compiler_params=_cparams(("parallel", "parallel", "arbitrary")), name="attention",
    )(qt, k, vt)


def _fourier_a_kernel(ab_ref, m1a_ref, m1b_ref, y_ref, *, n2c):
    for j in range(n2c):
        a = ab_ref[0, :, j * 2 * D_FOURIER: j * 2 * D_FOURIER + D_FOURIER]
        b = ab_ref[0, :, j * 2 * D_FOURIER + D_FOURIER: (j + 1) * 2 * D_FOURIER]
        y = _bdot(m1a_ref[...], a) + _bdot(m1b_ref[...], b)
        y_ref[0, :, j * D_FOURIER:(j + 1) * D_FOURIER] = y.astype(y_ref.dtype)


def _fourier_b_kernel(yr_ref, yi_ref, m2_ref, o_ref, *, k1c, n2):
    for j in range(k1c):
        m2 = m2_ref[j]
        z = _bdot(m2[:, :n2], yr_ref[0, j]) + _bdot(m2[:, n2:], yi_ref[0, j])
        o_ref[0, :, j, :] = z.astype(o_ref.dtype)


def _fourier(ab, m1a, m1b, m2):
    bsz, seq, _ = ab.shape
    n1 = DFT_N1
    n2 = seq // n1
    n2c = min(n2, 16)
    k1c = 8
    abv = ab.reshape(bsz, n1, n2 * 2 * D_FOURIER)
    y = pl.pallas_call(
        functools.partial(_fourier_a_kernel, n2c=n2c),
        grid=(bsz, n2 // n2c),
        in_specs=[
            pl.BlockSpec((1, n1, n2c * 2 * D_FOURIER), lambda b, c: (b, 0, c)),
            pl.BlockSpec(m1a.shape, lambda b, c: (0, 0)),
            pl.BlockSpec(m1b.shape, lambda b, c: (0, 0)),
        ],
        out_specs=pl.BlockSpec((1, 2 * n1, n2c * D_FOURIER), lambda b, c: (b, 0, c)),
        out_shape=jax.ShapeDtypeStruct((bsz, 2 * n1, n2 * D_FOURIER), jnp.bfloat16),
        compiler_params=_cparams(("parallel", "parallel")), name="fourier_a",
    )(abv, m1a, m1b)
    yv = y.reshape(bsz, 2 * n1, n2, D_FOURIER)
    nblk = n1 // k1c
    out = pl.pallas_call(
        functools.partial(_fourier_b_kernel, k1c=k1c, n2=n2),
        grid=(bsz, nblk),
        in_specs=[
            pl.BlockSpec((1, k1c, n2, D_FOURIER), lambda b, c: (b, c, 0, 0)),
            pl.BlockSpec((1, k1c, n2, D_FOURIER), lambda b, c: (b, nblk + c, 0, 0)),
            pl.BlockSpec((k1c, n2, 2 * n2), lambda b, c: (c, 0, 0)),
        ],
        out_specs=pl.BlockSpec((1, n2, k1c, D_FOURIER), lambda b, c: (b, 0, c, 0)),
        out_shape=jax.ShapeDtypeStruct((bsz, n2, n1, D_FOURIER), jnp.float32),
        compiler_params=_cparams(("parallel", "parallel")), name="fourier_b",
    )(yv, yv, m2)
    return out.reshape(bsz, seq, D_FOURIER)


def _dft_tables(seq):
    n1 = DFT_N1
    n2 = seq // n1
    c = np.arange(FOURIER_GROUP)
    ang = 2.0 * np.pi * np.outer(c, c) / FOURIER_GROUP
    eye = np.eye(D_FOURIER // FOURIER_GROUP)
    cc = np.kron(eye, np.cos(ang))
    sc = np.kron(eye, np.sin(ang))
    i1 = np.arange(n1)
    a1 = 2.0 * np.pi * np.outer(i1, i1) / n1
    c1, s1 = np.cos(a1), np.sin(a1)
    m1a = np.concatenate([c1, -s1], axis=0)
    m1b = np.concatenate([-s1, -c1], axis=0)
    k = i1[:, None] + n1 * np.arange(n2)[None, :]
    phi = 2.0 * np.pi * (k[:, :, None] * np.arange(n2)[None, None, :] % seq) / seq
    norm = 1.0 / math.sqrt(seq * FOURIER_GROUP)
    m2 = np.concatenate([np.cos(phi), np.sin(phi)], axis=-1) * norm
    bf = lambda t: jnp.asarray(t, jnp.float32).astype(jnp.bfloat16)
    return bf(cc), bf(sc), bf(m1a), bf(m1b), bf(m2)


def _outproj_kernel(o_ref, u_ref, up_ref, un_ref, b_ref, f_ref, h_ref, wom_ref, woc_ref, wof_ref, wconv_ref,
                    g_ref, beta_ref, wrh_ref, wrl_ref, br_ref, h1_ref, comb_ref, cnt_ref, *, alpha):
    i = pl.program_id(1)
    last = pl.num_programs(1) - 1
    tm = u_ref.shape[1]
    u = u_ref[0]
    up = up_ref[0][7:8, :] * (i > 0).astype(jnp.float32)
    un = un_ref[0][0:1, :] * (i < last).astype(jnp.float32)
    row = lax.broadcasted_iota(jnp.int32, u.shape, 0)
    u_m1 = jnp.where(row == 0, up, pltpu.roll(u, 1, axis=0))
    u_p1 = jnp.where(row == tm - 1, un, pltpu.roll(u, tm - 1, axis=0))
    wc = wconv_ref[...]
    oconv = b_ref[0] * (u_m1 * wc[0:1, :] + u * wc[1:2, :] + u_p1 * wc[2:3, :])
    mix = _bdot(oconv.astype(jnp.bfloat16), woc_ref[...]) + _bdot(f_ref[0].astype(jnp.bfloat16), wof_ref[...])
    mix = mix + lax.dot_general(o_ref[0], wom_ref[...], (((0,), (0,)), ((), ())),
                                preferred_element_type=jnp.float32)
    h1 = _layer_norm(alpha * h_ref[0] + mix, g_ref[...], beta_ref[...])
    h1_ref[0] = h1

    h_hi = h1.astype(jnp.bfloat16)
    h_lo = (h1 - h_hi.astype(jnp.float32)).astype(jnp.bfloat16)
    logits = (_bdot(h_hi, wrh_ref[...]) + (_bdot(h_hi, wrl_ref[...]) + _bdot(h_lo, wrh_ref[...]))) + br_ref[...]
    lane = lax.broadcasted_iota(jnp.int32, logits.shape, 1)
    neg = jnp.float32(-jnp.inf)
    big = jnp.int32(1 << 20)
    is_g = lane < N_GROUPS
    gl = jnp.where(is_g, logits, neg)
    gmax = jnp.max(gl, axis=-1, keepdims=True)
    p_group = 1.0 / jnp.sum(jnp.exp(gl - gmax), axis=-1, keepdims=True)
    g_sel = jnp.min(jnp.where(is_g & (logits == gmax), lane, big), axis=-1, keepdims=True)
    e_lo = N_GROUPS + g_sel * EXPERTS_PER_GROUP
    is_e = (lane >= e_lo) & (lane < e_lo + EXPERTS_PER_GROUP)
    l1 = jnp.max(jnp.where(is_e, logits, neg), axis=-1, keepdims=True)
    i1 = jnp.min(jnp.where(is_e & (logits == l1), lane, big), axis=-1, keepdims=True)
    is_e2 = is_e & (lane != i1)
    l2 = jnp.max(jnp.where(is_e2, logits, neg), axis=-1, keepdims=True)
    i2 = jnp.min(jnp.where(is_e2 & (logits == l2), lane, big), axis=-1, keepdims=True)
    r = jnp.exp(l2 - l1)
    w1 = 1.0 / (1.0 + r)
    w2 = r * w1
    comb = jnp.where(lane == i1, w1, jnp.where(lane == i2, w2, 0.0)) * p_group
    comb_ref[0] = jnp.where(lane == GID_LANE, g_sel.astype(jnp.float32), comb)
    cnt_ref[0, 0] = jnp.sum(jnp.where(lane == g_sel, 1.0, 0.0), axis=0, keepdims=True)


def _outproj(o, u, bgate, ofour, h, wom, woc, wof, wconv, g, beta, wrh, wrl, br, *, alpha, tm):
    bsz, seq, d = h.shape
    nblk8 = seq // 8
    r8 = tm // 8
    const2 = lambda b, i: (0, 0)
    const3 = lambda b, i: (0, 0, 0)
    tok = lambda w: pl.BlockSpec((1, tm, w), lambda b, i: (b, i, 0))
    return pl.pallas_call(
        functools.partial(_outproj_kernel, alpha=alpha),
        grid=(bsz, seq // tm),
        in_specs=[
            pl.BlockSpec((1, N_HEADS * V_DIM, tm), lambda b, i: (b, 0, i)),
            tok(D_CONV),
            pl.BlockSpec((1, 8, D_CONV), lambda b, i: (b, jnp.maximum(i * r8 - 1, 0), 0)),
            pl.BlockSpec((1, 8, D_CONV), lambda b, i: (b, jnp.minimum((i + 1) * r8, nblk8 - 1), 0)),
            tok(D_CONV),
            tok(D_FOURIER),
            tok(d),
            pl.BlockSpec(wom.shape, const2),
            pl.BlockSpec(woc.shape, const2),
            pl.BlockSpec(wof.shape, const2),
            pl.BlockSpec(wconv.shape, const2),
            pl.BlockSpec((1, d), const2),
            pl.BlockSpec((1, d), const2),
            pl.BlockSpec(wrh.shape, const2),
            pl.BlockSpec(wrl.shape, const2),
            pl.BlockSpec((1, ROUTER_LANES), const2),
        ],
        out_specs=[tok(d), tok(ROUTER_LANES), pl.BlockSpec((1, 1, 1, ROUTER_LANES), lambda b, i: (b, i, 0, 0))],
        out_shape=[jax.ShapeDtypeStruct((bsz, seq, d), jnp.float32),
                   jax.ShapeDtypeStruct((bsz, seq, ROUTER_LANES), jnp.float32),
                   jax.ShapeDtypeStruct((bsz, seq // tm, 1, ROUTER_LANES), jnp.float32)],
        compiler_params=_cparams(("parallel", "parallel")), name="outproj",
    )(o, u, u, u, bgate, ofour, h, wom, woc, wof, wconv, g, beta, wrh, wrl, br)


def _dot_tn(a, b):
    return lax.dot_general(a, b, (((0,), (0,)), ((), ())), preferred_element_type=jnp.float32)


def _moe_kernel(offs_ref, h_ref, comb_ref, wgu_ref, wd_ref, g_ref, beta_ref, o_ref, pt_ref, xs_ref, cw_ref, ys_ref,
                *, alpha):
    i = pl.program_id(0)
    s = pl.program_id(1)
    grp = s // MOE_SPLIT
    n_blk = MOE_TILE // MOE_BLK
    e_step = EXPERTS_PER_GROUP // MOE_SPLIT
    f32, bf16 = jnp.float32, jnp.bfloat16

    @pl.when(s == 0)
    def _():
        comb = comb_ref[...]
        lane = lax.broadcasted_iota(jnp.int32, comb.shape, 1)
        in_grp = lane.astype(f32) == comb[:, GID_LANE:GID_LANE + 1]
        row = lax.broadcasted_iota(jnp.int32, (MOE_TILE, MOE_TILE), 0)
        col = lax.broadcasted_iota(jnp.int32, (MOE_TILE, MOE_TILE), 1)
        lower = jnp.where(row >= col, 1.0, 0.0).astype(bf16)
        rank = _bdot(lower, jnp.where(in_grp, 1.0, 0.0).astype(bf16))
        lane1 = lax.broadcasted_iota(jnp.int32, (1, ROUTER_LANES), 1)
        start = jnp.zeros((1, ROUTER_LANES), f32)
        for gg in range(N_GROUPS):
            start = jnp.where(lane1 == gg, offs_ref[i, gg].astype(f32), start)
        pos = jnp.sum(jnp.where(in_grp, start + rank - 1.0, 0.0), axis=-1, keepdims=True)
        pt = jnp.where(col.astype(f32) == pos, 1.0, 0.0).astype(bf16)
        pt_ref[...] = pt
        xs = _dot_tn(h_ref[...].astype(bf16), pt)
        c1 = comb.astype(bf16)
        r1 = comb - c1.astype(f32)
        c2 = r1.astype(bf16)
        c3 = (r1 - c2.astype(f32)).astype(bf16)
        cw = _dot_tn(c1, pt) + _dot_tn(c2, pt) + _dot_tn(c3, pt)
        for bb in range(n_blk):
            xs_ref[bb] = xs[:, bb * MOE_BLK:(bb + 1) * MOE_BLK].astype(bf16)
            cw_ref[bb] = cw[:, bb * MOE_BLK:(bb + 1) * MOE_BLK]
        ys_ref[...] = jnp.zeros_like(ys_ref)

    lo = offs_ref[i, grp]
    hi = offs_ref[i, grp + 1]
    b_lo = lo // MOE_BLK
    b_hi = jnp.where(hi > lo, (hi + MOE_BLK - 1) // MOE_BLK, b_lo)
    cw_row = N_GROUPS + s * e_step

    def block(b, carry):
        au = _bdot(wgu_ref[0], xs_ref[b])
        hid = []
        for e in range(e_step):
            a = au[2 * e * D_EXPERT:(2 * e + 1) * D_EXPERT]
            up = au[(2 * e + 1) * D_EXPERT:(2 * e + 2) * D_EXPERT]
            w = cw_ref[b, pl.ds(cw_row + e, 1), :]
            hid.append((a * (1.0 / (1.0 + jnp.exp(-a))) * up * w).astype(bf16))
        ys_ref[b] += _bdot(wd_ref[0], jnp.concatenate(hid, axis=0))
        return carry

    lax.fori_loop(b_lo, b_hi, block, 0)

    @pl.when(s == pl.num_programs(1) - 1)
    def _():
        ys = jnp.concatenate([ys_ref[bb] for bb in range(n_blk)], axis=1)
        y_hi = ys.astype(bf16)
        y_lo = (ys - y_hi.astype(f32)).astype(bf16)
        pt = pt_ref[...]
        y = _dot_nt(pt, y_hi) + _dot_nt(pt, y_lo)
        o_ref[...] = _layer_norm(alpha * h_ref[...] + y, g_ref[...], beta_ref[...])


def _moe(h, comb, offs, wgu, wd, g, beta, *, alpha):
    n, d = h.shape
    n_steps = N_GROUPS * MOE_SPLIT
    const2 = lambda i, s, offs: (0, 0)
    grid_spec = pltpu.PrefetchScalarGridSpec(
        num_scalar_prefetch=1,
        grid=(n // MOE_TILE, n_steps),
        in_specs=[
            pl.BlockSpec((MOE_TILE, d), lambda i, s, offs: (i, 0), pipeline_mode=pl.Buffered(1)),
            pl.BlockSpec((MOE_TILE, ROUTER_LANES), lambda i, s, offs: (i, 0), pipeline_mode=pl.Buffered(1)),
            pl.BlockSpec((1,) + wgu.shape[1:], lambda i, s, offs: (s, 0, 0)),
            pl.BlockSpec((1,) + wd.shape[1:], lambda i, s, offs: (s, 0, 0)),
            pl.BlockSpec((1, d), const2),
            pl.BlockSpec((1, d), const2),
        ],
        out_specs=pl.BlockSpec((MOE_TILE, d), lambda i, s, offs: (i, 0)),
        scratch_shapes=[
            pltpu.VMEM((MOE_TILE, MOE_TILE), jnp.bfloat16),
            pltpu.VMEM((MOE_TILE // MOE_BLK, d, MOE_BLK), jnp.bfloat16),
            pltpu.VMEM((MOE_TILE // MOE_BLK, ROUTER_LANES, MOE_BLK), jnp.float32),
            pltpu.VMEM((MOE_TILE // MOE_BLK, d, MOE_BLK), jnp.float32),
        ],
    )
    return pl.pallas_call(
        functools.partial(_moe_kernel, alpha=alpha),
        grid_spec=grid_spec,
        out_shape=jax.ShapeDtypeStruct((n, d), jnp.float32),
        compiler_params=_cparams(("parallel", "arbitrary")), name="moe",
    )(offs, h, comb, wgu, wd, g, beta)


def _moe_weights(w_gate, w_up, w_down):
    d = w_gate.shape[-2]
    n_steps = N_GROUPS * MOE_SPLIT
    e_step = EXPERTS_PER_GROUP // MOE_SPLIT
    wg = jnp.swapaxes(w_gate.reshape(N_EXPERTS, d, D_EXPERT), 1, 2)
    wu = jnp.swapaxes(w_up.reshape(N_EXPERTS, d, D_EXPERT), 1, 2)
    wgu = jnp.stack([wg, wu], axis=1).reshape(n_steps, e_step * 2 * D_EXPERT, d)
    wd = w_down.reshape(n_steps, e_step, D_EXPERT, d)
    wd = jnp.transpose(wd, (0, 3, 1, 2)).reshape(n_steps, d, e_step * D_EXPERT)
    return wgu.astype(jnp.bfloat16), wd.astype(jnp.bfloat16)


def _group_offsets(cnt, tiles_per_moe_tile):
    c = cnt.reshape(-1, tiles_per_moe_tile, ROUTER_LANES)[:, :, :N_GROUPS].sum(axis=1)
    ends = jnp.cumsum(c, axis=-1)
    offs = jnp.concatenate([jnp.zeros_like(ends[:, :1]), ends], axis=-1)
    return jnp.pad(offs, ((0, 0), (0, 8 - offs.shape[1]))).astype(jnp.int32)


def _rope_tables(seq):
    inv_freq = 1.0 / (ROPE_THETA ** (jnp.arange(0, QK_ROPE, 2, dtype=jnp.float32) / QK_ROPE))
    ang = jnp.arange(seq, dtype=jnp.float32)[:, None] * inv_freq[None, :]
    cos, sin = jnp.cos(ang), jnp.sin(ang)
    c96 = jnp.concatenate([jnp.ones((seq, QK_NOPE), jnp.float32), cos, cos], axis=-1)
    s96 = jnp.concatenate([jnp.zeros((seq, QK_NOPE), jnp.float32), -sin, sin], axis=-1)
    return _pad_cols(c96, 0, K_LANES), _pad_cols(s96, 0, K_LANES), c96.T, s96.T


def _swap_halves(w):
    half = w.shape[-1] // 2
    return jnp.concatenate([w[..., half:], w[..., :half]], axis=-1)


def _pad_cols(w, left, total):
    return jnp.pad(w, [(0, 0)] * (w.ndim - 1) + [(left, total - left - w.shape[-1])])


def _pick_tile(n, want):
    t = min(n, want)
    while n % t:
        t //= 2
    return t


def kernel(x, ln_in_g, ln_in_b, w_in, g_q, g_kv, w_uq, w_ukv, w_conv, w_out, ln1_g, ln1_b, w_group, b_group,
           w_router, b_router, w_gate, w_up, w_down, ln2_g, ln2_b):
    bsz, seq, d = x.shape
    depth = w_in.shape[0]
    alpha = (2.0 * depth) ** 0.25
    bf16 = jnp.bfloat16
    tm = _pick_tile(seq, 512)
    tq = _pick_tile(seq, 1024)
    tk = tm

    c96, s96, c96t, s96t = _rope_tables(seq)
    cc, sc, m1a, m1b, m2 = _dft_tables(seq)
    row = lambda v: v.reshape(1, -1)
    d_mla = N_HEADS * V_DIM

    h = x
    for l in range(depth):
        wi = w_in[l]
        splits = np.cumsum([Q_LORA, KV_LORA, QK_ROPE, D_CONV, D_CONV, D_CONV])
        w_cq, w_ckv, w_kpe, w_b, w_c, w_h, w_f = jnp.split(wi, splits, axis=-1)
        win = jnp.concatenate([
            w_cq, w_ckv, w_b, w_c, w_h, w_f,
            _pad_cols(w_kpe, QK_NOPE, 128), _pad_cols(_swap_halves(w_kpe), QK_NOPE, 128)], axis=-1).astype(bf16)
        wq = jnp.transpose(w_uq[l], (1, 0, 2))
        wqs = _pad_cols(_swap_halves(wq[..., QK_NOPE:]), QK_NOPE, D_QK)
        wqt = jnp.transpose(wq, (0, 2, 1)).reshape(N_HEADS * D_QK, Q_LORA)
        wqst = jnp.transpose(wqs, (0, 2, 1)).reshape(N_HEADS * D_QK, Q_LORA)
        wkv = jnp.transpose(w_ukv[l], (1, 0, 2))
        wk = _pad_cols(wkv[..., :QK_NOPE], 0, K_LANES)
        wvt = _pad_cols(wkv[..., QK_NOPE:], 0, V_ROWS)
        wvt = jnp.transpose(wvt, (0, 2, 1)).reshape(N_HEADS * V_ROWS, KV_LORA)
        outs = _inproj(h, row(ln_in_g), row(ln_in_b), win, row(g_q[l]), row(g_kv[l]), wqt.astype(bf16),
                       wqst.astype(bf16), wk.astype(bf16), wvt.astype(bf16), c96, s96, c96t, s96t, cc, sc,
                       ln_input=(l == 0), tm=tm)
        if l == 0:
            h, outs = outs[0], outs[1:]
        qt, k, vt, u, bgate, ab = outs
        o_mla = _attention(qt, k, vt, tq=tq, tk=tk)
        o_four = _fourier(ab, m1a, m1b, m2)

        wo = w_out[l].astype(bf16)
        wom = wo[:d_mla]
        woc = wo[d_mla:d_mla + D_CONV]
        wof = wo[d_mla + D_CONV:]
        wr = _pad_cols(jnp.concatenate([w_group[l], w_router[l].reshape(d, N_EXPERTS)], axis=-1), 0, ROUTER_LANES)
        br = _pad_cols(jnp.concatenate([b_group[l], b_router[l].reshape(N_EXPERTS)])[None, :], 0, ROUTER_LANES)
        wrh = wr.astype(bf16)
        wrl = (wr - wrh.astype(jnp.float32)).astype(bf16)
        h1, comb, cnt = _outproj(o_mla, u, bgate, o_four, h, wom, woc, wof, w_conv[l], row(ln1_g[l]), row(ln1_b[l]),
                            wrh, wrl, br, alpha=alpha, tm=tm)

        wgu, wd = _moe_weights(w_gate[l], w_up[l], w_down[l])
        offs = _group_offsets(cnt, MOE_TILE // tm)
        h = _moe(h1.reshape(bsz * seq, d), comb.reshape(bsz * seq, ROUTER_LANES), offs, wgu, wd,
                 row(ln2_g[l]), row(ln2_b[l]), alpha=alpha).reshape(bsz, seq, d)
    return h
```

```python
import functools
import math

import numpy as np
import jax
import jax.numpy as jnp
from jax import lax
from jax.experimental import pallas as pl
from jax.experimental.pallas import tpu as pltpu

N_HEADS = 8
QK_NOPE = 64
QK_ROPE = 32
D_QK = QK_NOPE + QK_ROPE
K_LANES = 128
V_DIM = 64
V_ROWS = 80
Q_LORA = 384
KV_LORA = 256
D_CONV = 256
D_FOURIER = 256
FOURIER_GROUP = 64
N_GROUPS = 4
EXPERTS_PER_GROUP = 8
N_EXPERTS = N_GROUPS * EXPERTS_PER_GROUP
D_EXPERT = 256
ROPE_THETA = 10000.0
LN_EPS = 1e-5
RMS_EPS = 1e-6
DFT_N1 = 128
ROUTER_LANES = 128
GID_LANE = N_GROUPS + N_EXPERTS
MOE_TILE = 1024
MOE_BLK = 256
VMEM_LIMIT = 56 * 1024 * 1024
ATTN_SCORE_BUF_BYTES = 8 * 1024 * 1024

_O_CQ = 0
_O_CKV = _O_CQ + Q_LORA
_O_B = _O_CKV + KV_LORA
_O_C = _O_B + D_CONV
_O_H = _O_C + D_CONV
_O_F = _O_H + D_CONV
_O_KPE = _O_F + D_FOURIER
_O_KPES = _O_KPE + 128
_W_IN_COLS = _O_KPES + 128


def _cparams(sem):
    return pltpu.CompilerParams(dimension_semantics=sem, vmem_limit_bytes=VMEM_LIMIT)


def _layer_norm(x, g, b):
    mu = jnp.mean(x, axis=-1, keepdims=True)
    xc = x - mu
    var = jnp.mean(xc * xc, axis=-1, keepdims=True)
    return xc * lax.rsqrt(var + LN_EPS) * g + b


def _rms_norm(x, g):
    return x * lax.rsqrt(jnp.mean(x * x, axis=-1, keepdims=True) + RMS_EPS) * g


def _bdot(a, b):
    return jnp.dot(a, b, preferred_element_type=jnp.float32)


def _dot_nt(a, b):
    return lax.dot_general(a, b, (((1,), (1,)), ((), ())), preferred_element_type=jnp.float32)


def _inproj_kernel(x_ref, lng_ref, lnb_ref, win_ref, gq_ref, gkv_ref, wqt_ref, wqst_ref, wk_ref, wvt_ref,
                   c96_ref, s96_ref, c96t_ref, s96t_ref, cc_ref, sc_ref, *out_refs, ln_input, q_scale):
    if ln_input:
        h_ref, qt_ref, k_ref, vt_ref, u_ref, b_ref, ab_ref = out_refs
        h = _layer_norm(x_ref[0], lng_ref[...], lnb_ref[...])
        h_ref[0] = h
    else:
        qt_ref, k_ref, vt_ref, u_ref, b_ref, ab_ref = out_refs
        h = x_ref[0]
    z = _bdot(h.astype(jnp.bfloat16), win_ref[...])
    c96 = c96_ref[...]
    s96 = s96_ref[...]
    c96t = c96t_ref[...]
    s96t = s96t_ref[...]
    cqn = _rms_norm(z[:, _O_CQ:_O_CQ + Q_LORA], gq_ref[...]).astype(jnp.bfloat16)
    ckvn = _rms_norm(z[:, _O_CKV:_O_CKV + KV_LORA], gkv_ref[...]).astype(jnp.bfloat16)
    kpe = z[:, _O_KPE:_O_KPE + K_LANES] * c96 + z[:, _O_KPES:_O_KPES + K_LANES] * s96
    qt_all = _dot_nt(wqt_ref[...], cqn)
    qst_all = _dot_nt(wqst_ref[...], cqn)
    vt_all = _dot_nt(wvt_ref[...], ckvn)
    vrow = lax.broadcasted_iota(jnp.int32, (V_ROWS, vt_all.shape[1]), 0)
    for hd in range(N_HEADS):
        qt = (qt_all[hd * D_QK:(hd + 1) * D_QK] * c96t + qst_all[hd * D_QK:(hd + 1) * D_QK] * s96t) * q_scale
        qt_ref[0, hd] = qt.astype(qt_ref.dtype)
        k_ref[0, hd] = (_bdot(ckvn, wk_ref[hd]) + kpe).astype(k_ref.dtype)
        vt = jnp.where(vrow == V_DIM, 1.0, vt_all[hd * V_ROWS:(hd + 1) * V_ROWS])
        vt_ref[0, hd, 0] = vt.astype(vt_ref.dtype)
    u_ref[0] = z[:, _O_C:_O_C + D_CONV] * z[:, _O_H:_O_H + D_CONV]
    b_ref[0] = z[:, _O_B:_O_B + D_CONV]
    fb = z[:, _O_F:_O_F + D_FOURIER].astype(jnp.bfloat16)
    ab_ref[0, :, :D_FOURIER] = _bdot(fb, cc_ref[...]).astype(ab_ref.dtype)
    ab_ref[0, :, D_FOURIER:] = _bdot(fb, sc_ref[...]).astype(ab_ref.dtype)


def _inproj(x, lng, lnb, win, gq, gkv, wqt, wqst, wk, wvt, c96, s96, c96t, s96t, cc, sc, *, ln_input, tm, tk):
    bsz, seq, d = x.shape
    q_scale = (D_QK ** -0.5) * math.log2(math.e)
    grid = (bsz, seq // tm)
    const2 = lambda b, i: (0, 0)
    const3 = lambda b, i: (0, 0, 0)
    in_specs = [
        pl.BlockSpec((1, tm, d), lambda b, i: (b, i, 0)),
        pl.BlockSpec((1, d), const2),
        pl.BlockSpec((1, d), const2),
        pl.BlockSpec(win.shape, const2),
        pl.BlockSpec((1, Q_LORA), const2),
        pl.BlockSpec((1, KV_LORA), const2),
        pl.BlockSpec(wqt.shape, const2),
        pl.BlockSpec(wqst.shape, const2),
        pl.BlockSpec(wk.shape, const3),
        pl.BlockSpec(wvt.shape, const2),
        pl.BlockSpec((tm, K_LANES), lambda b, i: (i, 0)),
        pl.BlockSpec((tm, K_LANES), lambda b, i: (i, 0)),
        pl.BlockSpec((D_QK, tm), lambda b, i: (0, i)),
        pl.BlockSpec((D_QK, tm), lambda b, i: (0, i)),
        pl.BlockSpec(cc.shape, const2),
        pl.BlockSpec(sc.shape, const2),
    ]
    tok_spec = lambda w: pl.BlockSpec((1, tm, w), lambda b, i: (b, i, 0))
    out_shape = [
        jax.ShapeDtypeStruct((bsz, N_HEADS, D_QK, seq), jnp.bfloat16),
        jax.ShapeDtypeStruct((bsz, N_HEADS, seq, K_LANES), jnp.bfloat16),
        jax.ShapeDtypeStruct((bsz, N_HEADS, seq // tk, V_ROWS, tk), jnp.bfloat16),
        jax.ShapeDtypeStruct((bsz, seq, D_CONV), jnp.float32),
        jax.ShapeDtypeStruct((bsz, seq, D_CONV), jnp.float32),
        jax.ShapeDtypeStruct((bsz, seq, 2 * D_FOURIER), jnp.bfloat16),
    ]
    out_specs = [pl.BlockSpec((1, N_HEADS, D_QK, tm), lambda b, i: (b, 0, 0, i)),
                 pl.BlockSpec((1, N_HEADS, tm, K_LANES), lambda b, i: (b, 0, i, 0)),
                 pl.BlockSpec((1, N_HEADS, 1, V_ROWS, tm), lambda b, i: (b, 0, i // (tk // tm), 0, i % (tk // tm))),
                 tok_spec(D_CONV), tok_spec(D_CONV), tok_spec(2 * D_FOURIER)]
    if ln_input:
        out_shape = [jax.ShapeDtypeStruct((bsz, seq, d), jnp.float32)] + out_shape
        out_specs = [tok_spec(d)] + out_specs
    return pl.pallas_call(
        functools.partial(_inproj_kernel, ln_input=ln_input, q_scale=q_scale),
        grid=grid, in_specs=in_specs, out_specs=out_specs, out_shape=out_shape,
        compiler_params=_cparams(("parallel", "parallel")), name="inproj",
    )(x, lng, lnb, win, gq, gkv, wqt, wqst, wk, wvt, c96, s96, c96t, s96t, cc, sc)


def _attn_kernel(qt_ref, k_ref, vt_ref, o_ref, s0_ref, s1_ref, c0_ref, c1_ref, m_ref, acc_ref, *, tk, n_kv):
    qt = qt_ref[0, 0]

    def scores(j, s_ref, c_ref):
        off = pl.multiple_of(j * tk, tk)
        s = _bdot(k_ref[0, 0, pl.ds(off, tk), :D_QK], qt)
        s_ref[...] = s
        c_ref[...] = jnp.max(s, axis=0, keepdims=True)

    def update(j, s_ref, c_ref):
        s = s_ref[...]
        m_old = m_ref[...]
        m_new = jnp.maximum(m_old, c_ref[...])
        p = jnp.exp2(s - m_new).astype(jnp.bfloat16)
        acc_ref[...] = jnp.exp2(m_old - m_new) * acc_ref[...] + _bdot(vt_ref[0, 0, j], p)
        m_ref[...] = m_new

    m_ref[...] = jnp.full(m_ref.shape, -jnp.inf, jnp.float32)
    acc_ref[...] = jnp.zeros(acc_ref.shape, jnp.float32)
    scores(0, s0_ref, c0_ref)

    def pair(i, carry):
        j = 2 * i
        scores(j + 1, s1_ref, c1_ref)
        update(j, s0_ref, c0_ref)
        scores(j + 2, s0_ref, c0_ref)
        update(j + 1, s1_ref, c1_ref)
        return carry

    lax.fori_loop(0, n_kv // 2 - 1, pair, 0)
    scores(n_kv - 1, s1_ref, c1_ref)
    update(n_kv - 2, s0_ref, c0_ref)
    update(n_kv - 1, s1_ref, c1_ref)

    acc = acc_ref[...]
    o_ref[0] = (acc[:V_DIM] / acc[V_DIM:V_DIM + 1]).astype(o_ref.dtype)


def _attention(qt, k, vt, *, tq):
    bsz, nh, seq, _ = k.shape
    n_kv, _, tk = vt.shape[2:]
    assert n_kv % 2 == 0 and n_kv * tk == seq
    return pl.pallas_call(
        functools.partial(_attn_kernel, tk=tk, n_kv=n_kv),
        grid=(bsz, nh, seq // tq),
        in_specs=[
            pl.BlockSpec((1, 1, D_QK, tq), lambda b, h, i: (b, h, 0, i)),
            pl.BlockSpec((1, 1, seq, K_LANES), lambda b, h, i: (b, h, 0, 0)),
            pl.BlockSpec((1, 1, n_kv, V_ROWS, tk), lambda b, h, i: (b, h, 0, 0, 0)),
        ],
        out_specs=pl.BlockSpec((1, V_DIM, tq), lambda b, h, i: (b, h, i)),
        out_shape=jax.ShapeDtypeStruct((bsz, nh * V_DIM, seq), jnp.bfloat16),
        scratch_shapes=[pltpu.VMEM((tk, tq), jnp.float32), pltpu.VMEM((tk, tq), jnp.float32),
                        pltpu.VMEM((1, tq), jnp.float32), pltpu.VMEM((1, tq), jnp.float32),
                        pltpu.VMEM((1, tq), jnp.float32), pltpu.VMEM((V_ROWS, tq), jnp.float32)],
        compiler_params=_cparams(("parallel", "parallel", "arbitrary")), name="attention",
    )(qt, k, vt)


def _attn_kv_chunk(seq, tm, tq):
    tk = tm
    while seq % (4 * tk) == 0 and 2 * tk * tq * 4 <= ATTN_SCORE_BUF_BYTES:
        tk *= 2
    return tk


def _fourier_a_kernel(ab_ref, m1a_ref, m1b_ref, y_ref, *, n2c):
    for j in range(n2c):
        a = ab_ref[0, :, j * 2 * D_FOURIER: j * 2 * D_FOURIER + D_FOURIER]
        b = ab_ref[0, :, j * 2 * D_FOURIER + D_FOURIER: (j + 1) * 2 * D_FOURIER]
        y = _bdot(m1a_ref[...], a) + _bdot(m1b_ref[...], b)
        y_ref[0, :, j * D_FOURIER:(j + 1) * D_FOURIER] = y.astype(y_ref.dtype)


def _fourier_b_kernel(yr_ref, yi_ref, m2_ref, o_ref, *, k1c, n2):
    for j in range(k1c):
        m2 = m2_ref[j]
        z = _bdot(m2[:, :n2], yr_ref[0, j]) + _bdot(m2[:, n2:], yi_ref[0, j])
        o_ref[0, :, j, :] = z.astype(o_ref.dtype)


def _fourier(ab, m1a, m1b, m2):
    bsz, seq, _ = ab.shape
    n1 = DFT_N1
    n2 = seq // n1
    n2c = min(n2, 16)
    k1c = 8
    abv = ab.reshape(bsz, n1, n2 * 2 * D_FOURIER)
    y = pl.pallas_call(
        functools.partial(_fourier_a_kernel, n2c=n2c),
        grid=(bsz, n2 // n2c),
        in_specs=[
            pl.BlockSpec((1, n1, n2c * 2 * D_FOURIER), lambda b, c: (b, 0, c)),
            pl.BlockSpec(m1a.shape, lambda b, c: (0, 0)),
            pl.BlockSpec(m1b.shape, lambda b, c: (0, 0)),
        ],
        out_specs=pl.BlockSpec((1, 2 * n1, n2c * D_FOURIER), lambda b, c: (b, 0, c)),
        out_shape=jax.ShapeDtypeStruct((bsz, 2 * n1, n2 * D_FOURIER), jnp.bfloat16),
        compiler_params=_cparams(("parallel", "parallel")), name="fourier_a",
    )(abv, m1a, m1b)
    yv = y.reshape(bsz, 2 * n1, n2, D_FOURIER)
    nblk = n1 // k1c
    out = pl.pallas_call(
        functools.partial(_fourier_b_kernel, k1c=k1c, n2=n2),
        grid=(bsz, nblk),
        in_specs=[
            pl.BlockSpec((1, k1c, n2, D_FOURIER), lambda b, c: (b, c, 0, 0)),
            pl.BlockSpec((1, k1c, n2, D_FOURIER), lambda b, c: (b, nblk + c, 0, 0)),
            pl.BlockSpec((k1c, n2, 2 * n2), lambda b, c: (c, 0, 0)),
        ],
        out_specs=pl.BlockSpec((1, n2, k1c, D_FOURIER), lambda b, c: (b, 0, c, 0)),
        out_shape=jax.ShapeDtypeStruct((bsz, n2, n1, D_FOURIER), jnp.float32),
        compiler_params=_cparams(("parallel", "parallel")), name="fourier_b",
    )(yv, yv, m2)
    return out.reshape(bsz, seq, D_FOURIER)


def _dft_tables(seq):
    n1 = DFT_N1
    n2 = seq // n1
    c = np.arange(FOURIER_GROUP)
    ang = 2.0 * np.pi * np.outer(c, c) / FOURIER_GROUP
    eye = np.eye(D_FOURIER // FOURIER_GROUP)
    cc = np.kron(eye, np.cos(ang))
    sc = np.kron(eye, np.sin(ang))
    i1 = np.arange(n1)
    a1 = 2.0 * np.pi * np.outer(i1, i1) / n1
    c1, s1 = np.cos(a1), np.sin(a1)
    m1a = np.concatenate([c1, -s1], axis=0)
    m1b = np.concatenate([-s1, -c1], axis=0)
    k = i1[:, None] + n1 * np.arange(n2)[None, :]
    phi = 2.0 * np.pi * (k[:, :, None] * np.arange(n2)[None, None, :] % seq) / seq
    norm = 1.0 / math.sqrt(seq * FOURIER_GROUP)
    m2 = np.concatenate([np.cos(phi), np.sin(phi)], axis=-1) * norm
    bf = lambda t: jnp.asarray(t, jnp.float32).astype(jnp.bfloat16)
    return bf(cc), bf(sc), bf(m1a), bf(m1b), bf(m2)


def _outproj_kernel(o_ref, u_ref, up_ref, un_ref, b_ref, f_ref, h_ref, wom_ref, woc_ref, wof_ref, wconv_ref,
                    g_ref, beta_ref, wrh_ref, wrl_ref, br_ref, h1_ref, comb_ref, cnt_ref, *, alpha):
    i = pl.program_id(1)
    last = pl.num_programs(1) - 1
    tm = u_ref.shape[1]
    u = u_ref[0]
    up = up_ref[0][7:8, :] * (i > 0).astype(jnp.float32)
    un = un_ref[0][0:1, :] * (i < last).astype(jnp.float32)
    row = lax.broadcasted_iota(jnp.int32, u.shape, 0)
    u_m1 = jnp.where(row == 0, up, pltpu.roll(u, 1, axis=0))
    u_p1 = jnp.where(row == tm - 1, un, pltpu.roll(u, tm - 1, axis=0))
    wc = wconv_ref[...]
    oconv = b_ref[0] * (u_m1 * wc[0:1, :] + u * wc[1:2, :] + u_p1 * wc[2:3, :])
    mix = _bdot(oconv.astype(jnp.bfloat16), woc_ref[...]) + _bdot(f_ref[0].astype(jnp.bfloat16), wof_ref[...])
    mix = mix + lax.dot_general(o_ref[0], wom_ref[...], (((0,), (0,)), ((), ())),
                                preferred_element_type=jnp.float32)
    h1 = _layer_norm(alpha * h_ref[0] + mix, g_ref[...], beta_ref[...])
    h1_ref[0] = h1

    h_hi = h1.astype(jnp.bfloat16)
    h_lo = (h1 - h_hi.astype(jnp.float32)).astype(jnp.bfloat16)
    logits = (_bdot(h_hi, wrh_ref[...]) + (_bdot(h_hi, wrl_ref[...]) + _bdot(h_lo, wrh_ref[...]))) + br_ref[...]
    lane = lax.broadcasted_iota(jnp.int32, logits.shape, 1)
    neg = jnp.float32(-jnp.inf)
    big = jnp.int32(1 << 20)
    is_g = lane < N_GROUPS
    gl = jnp.where(is_g, logits, neg)
    gmax = jnp.max(gl, axis=-1, keepdims=True)
    p_group = 1.0 / jnp.sum(jnp.exp(gl - gmax), axis=-1, keepdims=True)
    g_sel = jnp.min(jnp.where(is_g & (logits == gmax), lane, big), axis=-1, keepdims=True)
    e_lo = N_GROUPS + g_sel * EXPERTS_PER_GROUP
    is_e = (lane >= e_lo) & (lane < e_lo + EXPERTS_PER_GROUP)
    l1 = jnp.max(jnp.where(is_e, logits, neg), axis=-1, keepdims=True)
    i1 = jnp.min(jnp.where(is_e & (logits == l1), lane, big), axis=-1, keepdims=True)
    is_e2 = is_e & (lane != i1)
    l2 = jnp.max(jnp.where(is_e2, logits, neg), axis=-1, keepdims=True)
    i2 = jnp.min(jnp.where(is_e2 & (logits == l2), lane, big), axis=-1, keepdims=True)
    r = jnp.exp(l2 - l1)
    w1 = 1.0 / (1.0 + r)
    w2 = r * w1
    comb = jnp.where(lane == i1, w1, jnp.where(lane == i2, w2, 0.0)) * p_group
    comb_ref[0] = jnp.where(lane == GID_LANE, g_sel.astype(jnp.float32), comb)
    cnt_ref[0, 0] = jnp.sum(jnp.where(lane == g_sel, 1.0, 0.0), axis=0, keepdims=True)


def _outproj(o, u, bgate, ofour, h, wom, woc, wof, wconv, g, beta, wrh, wrl, br, *, alpha, tm):
    bsz, seq, d = h.shape
    nblk8 = seq // 8
    r8 = tm // 8
    const2 = lambda b, i: (0, 0)
    const3 = lambda b, i: (0, 0, 0)
    tok = lambda w: pl.BlockSpec((1, tm, w), lambda b, i: (b, i, 0))
    return pl.pallas_call(
        functools.partial(_outproj_kernel, alpha=alpha),
        grid=(bsz, seq // tm),
        in_specs=[
            pl.BlockSpec((1, N_HEADS * V_DIM, tm), lambda b, i: (b, 0, i)),
            tok(D_CONV),
            pl.BlockSpec((1, 8, D_CONV), lambda b, i: (b, jnp.maximum(i * r8 - 1, 0), 0)),
            pl.BlockSpec((1, 8, D_CONV), lambda b, i: (b, jnp.minimum((i + 1) * r8, nblk8 - 1), 0)),
            tok(D_CONV),
            tok(D_FOURIER),
            tok(d),
            pl.BlockSpec(wom.shape, const2),
            pl.BlockSpec(woc.shape, const2),
            pl.BlockSpec(wof.shape, const2),
            pl.BlockSpec(wconv.shape, const2),
            pl.BlockSpec((1, d), const2),
            pl.BlockSpec((1, d), const2),
            pl.BlockSpec(wrh.shape, const2),
            pl.BlockSpec(wrl.shape, const2),
            pl.BlockSpec((1, ROUTER_LANES), const2),
        ],
        out_specs=[tok(d), tok(ROUTER_LANES), pl.BlockSpec((1, 1, 1, ROUTER_LANES), lambda b, i: (b, i, 0, 0))],
        out_shape=[jax.ShapeDtypeStruct((bsz, seq, d), jnp.float32),
                   jax.ShapeDtypeStruct((bsz, seq, ROUTER_LANES), jnp.float32),
                   jax.ShapeDtypeStruct((bsz, seq // tm, 1, ROUTER_LANES), jnp.float32)],
        compiler_params=_cparams(("parallel", "parallel")), name="outproj",
    )(o, u, u, u, bgate, ofour, h, wom, woc, wof, wconv, g, beta, wrh, wrl, br)


def _dot_tn(a, b):
    return lax.dot_general(a, b, (((0,), (0,)), ((), ())), preferred_element_type=jnp.float32)


def _moe_kernel(offs_ref, h_ref, comb_ref, wgu_ref, wd_ref, g_ref, beta_ref, o_ref, pt_ref, xs_ref, cw_ref, ys_ref,
                *, alpha):
    i = pl.program_id(0)
    s = pl.program_id(1)
    grp = s
    n_blk = MOE_TILE // MOE_BLK
    e_step = EXPERTS_PER_GROUP
    f32, bf16 = jnp.float32, jnp.bfloat16

    @pl.when(s == 0)
    def _():
        comb = comb_ref[...]
        lane = lax.broadcasted_iota(jnp.int32, comb.shape, 1)
        in_grp = lane.astype(f32) == comb[:, GID_LANE:GID_LANE + 1]
        row = lax.broadcasted_iota(jnp.int32, (MOE_TILE, MOE_TILE), 0)
        col = lax.broadcasted_iota(jnp.int32, (MOE_TILE, MOE_TILE), 1)
        lower = jnp.where(row >= col, 1.0, 0.0).astype(bf16)
        rank = _bdot(lower, jnp.where(in_grp, 1.0, 0.0).astype(bf16))
        lane1 = lax.broadcasted_iota(jnp.int32, (1, ROUTER_LANES), 1)
        start = jnp.zeros((1, ROUTER_LANES), f32)
        for gg in range(N_GROUPS):
            start = jnp.where(lane1 == gg, offs_ref[i, gg].astype(f32), start)
        pos = jnp.sum(jnp.where(in_grp, start + rank - 1.0, 0.0), axis=-1, keepdims=True)
        pt = jnp.where(col.astype(f32) == pos, 1.0, 0.0).astype(bf16)
        pt_ref[...] = pt
        c1 = comb.astype(bf16)
        r1 = comb - c1.astype(f32)
        c2 = r1.astype(bf16)
        c3 = (r1 - c2.astype(f32)).astype(bf16)
        d = h_ref.shape[1]
        moved = _dot_tn(jnp.concatenate([h_ref[...].astype(bf16), c1, c2, c3], axis=1), pt)
        xs = moved[:d]
        cw = (moved[d:d + ROUTER_LANES] + moved[d + ROUTER_LANES:d + 2 * ROUTER_LANES]
              + moved[d + 2 * ROUTER_LANES:])
        for bb in range(n_blk):
            xs_ref[bb] = xs[:, bb * MOE_BLK:(bb + 1) * MOE_BLK].astype(bf16)
            cw_ref[bb] = cw[:, bb * MOE_BLK:(bb + 1) * MOE_BLK]
        ys_ref[...] = jnp.zeros_like(ys_ref)

    lo = offs_ref[i, grp]
    hi = offs_ref[i, grp + 1]
    b_lo = lo // MOE_BLK
    b_hi = jnp.where(hi > lo, (hi + MOE_BLK - 1) // MOE_BLK, b_lo)
    cw_row = N_GROUPS + s * e_step

    def block(b, carry):
        au = _bdot(wgu_ref[0], xs_ref[b])
        hid = []
        for e in range(e_step):
            a = au[2 * e * D_EXPERT:(2 * e + 1) * D_EXPERT]
            up = au[(2 * e + 1) * D_EXPERT:(2 * e + 2) * D_EXPERT]
            w = cw_ref[b, pl.ds(cw_row + e, 1), :]
            hid.append((a * (1.0 / (1.0 + jnp.exp(-a))) * up * w).astype(bf16))
        ys_ref[b] += _bdot(wd_ref[0], jnp.concatenate(hid, axis=0))
        return carry

    lax.fori_loop(b_lo, b_hi, block, 0)

    @pl.when(s == pl.num_programs(1) - 1)
    def _():
        ys = jnp.concatenate([ys_ref[bb] for bb in range(n_blk)], axis=1)
        y_hi = ys.astype(bf16)
        y_lo = (ys - y_hi.astype(f32)).astype(bf16)
        pt = pt_ref[...]
        y = _dot_nt(pt, y_hi) + _dot_nt(pt, y_lo)
        o_ref[...] = _layer_norm(alpha * h_ref[...] + y, g_ref[...], beta_ref[...])


def _moe(h, comb, offs, wgu, wd, g, beta, *, alpha):
    n, d = h.shape
    const2 = lambda i, s, offs: (0, 0)
    grid_spec = pltpu.PrefetchScalarGridSpec(
        num_scalar_prefetch=1,
        grid=(n // MOE_TILE, N_GROUPS),
        in_specs=[
            pl.BlockSpec((MOE_TILE, d), lambda i, s, offs: (i, 0), pipeline_mode=pl.Buffered(1)),
            pl.BlockSpec((MOE_TILE, ROUTER_LANES), lambda i, s, offs: (i, 0), pipeline_mode=pl.Buffered(1)),
            pl.BlockSpec((1,) + wgu.shape[1:], lambda i, s, offs: (s, 0, 0)),
            pl.BlockSpec((1,) + wd.shape[1:], lambda i, s, offs: (s, 0, 0)),
            pl.BlockSpec((1, d), const2),
            pl.BlockSpec((1, d), const2),
        ],
        out_specs=pl.BlockSpec((MOE_TILE, d), lambda i, s, offs: (i, 0)),
        scratch_shapes=[
            pltpu.VMEM((MOE_TILE, MOE_TILE), jnp.bfloat16),
            pltpu.VMEM((MOE_TILE // MOE_BLK, d, MOE_BLK), jnp.bfloat16),
            pltpu.VMEM((MOE_TILE // MOE_BLK, ROUTER_LANES, MOE_BLK), jnp.float32),
            pltpu.VMEM((MOE_TILE // MOE_BLK, d, MOE_BLK), jnp.float32),
        ],
    )
    return pl.pallas_call(
        functools.partial(_moe_kernel, alpha=alpha),
        grid_spec=grid_spec,
        out_shape=jax.ShapeDtypeStruct((n, d), jnp.float32),
        compiler_params=_cparams(("parallel", "arbitrary")), name="moe",
    )(offs, h, comb, wgu, wd, g, beta)


def _moe_weights_kernel(wg_ref, wu_ref, wd_ref, wgu_ref, wdt_ref):
    wgu_ref[0, 0] = wg_ref[0, 0, 0].T.astype(wgu_ref.dtype)
    wgu_ref[0, 1] = wu_ref[0, 0, 0].T.astype(wgu_ref.dtype)
    wdt_ref[0] = wd_ref[0, 0, 0].T.astype(wdt_ref.dtype)


def _moe_weights(w_gate, w_up, w_down, layer):
    _, ng, ne, d, f = w_gate.shape
    w_spec = lambda a, b: pl.BlockSpec((1, 1, 1, a, b), lambda g, e: (layer, g, e, 0, 0))
    wgu, wdt = pl.pallas_call(
        _moe_weights_kernel,
        grid=(ng, ne),
        in_specs=[w_spec(d, f), w_spec(d, f), w_spec(f, d)],
        out_specs=[pl.BlockSpec((1, 2, f, d), lambda g, e: (g * ne + e, 0, 0, 0)),
                   pl.BlockSpec((1, d, f), lambda g, e: (g, 0, e))],
        out_shape=[jax.ShapeDtypeStruct((ng * ne, 2, f, d), jnp.bfloat16),
                   jax.ShapeDtypeStruct((ng, d, ne * f), jnp.bfloat16)],
        compiler_params=_cparams(("parallel", "parallel")), name="moe_weights",
    )(w_gate, w_up, w_down)
    return wgu.reshape(ng, ne * 2 * f, d), wdt


def _group_offsets(cnt, tiles_per_moe_tile):
    c = cnt.reshape(-1, tiles_per_moe_tile, ROUTER_LANES)[:, :, :N_GROUPS].sum(axis=1)
    ends = jnp.cumsum(c, axis=-1)
    offs = jnp.concatenate([jnp.zeros_like(ends[:, :1]), ends], axis=-1)
    return jnp.pad(offs, ((0, 0), (0, 8 - offs.shape[1]))).astype(jnp.int32)


def _rope_tables(seq):
    inv_freq = 1.0 / (ROPE_THETA ** (jnp.arange(0, QK_ROPE, 2, dtype=jnp.float32) / QK_ROPE))
    ang = jnp.arange(seq, dtype=jnp.float32)[:, None] * inv_freq[None, :]
    cos, sin = jnp.cos(ang), jnp.sin(ang)
    c96 = jnp.concatenate([jnp.ones((seq, QK_NOPE), jnp.float32), cos, cos], axis=-1)
    s96 = jnp.concatenate([jnp.zeros((seq, QK_NOPE), jnp.float32), -sin, sin], axis=-1)
    return _pad_cols(c96, 0, K_LANES), _pad_cols(s96, 0, K_LANES), c96.T, s96.T


def _swap_halves(w):
    half = w.shape[-1] // 2
    return jnp.concatenate([w[..., half:], w[..., :half]], axis=-1)


def _pad_cols(w, left, total):
    return jnp.pad(w, [(0, 0)] * (w.ndim - 1) + [(left, total - left - w.shape[-1])])


def _pick_tile(n, want):
    t = min(n, want)
    while n % t:
        t //= 2
    return t


def kernel(x, ln_in_g, ln_in_b, w_in, g_q, g_kv, w_uq, w_ukv, w_conv, w_out, ln1_g, ln1_b, w_group, b_group,
           w_router, b_router, w_gate, w_up, w_down, ln2_g, ln2_b):
    bsz, seq, d = x.shape
    depth = w_in.shape[0]
    alpha = (2.0 * depth) ** 0.25
    bf16 = jnp.bfloat16
    tm = _pick_tile(seq, 512)
    tq = _pick_tile(seq, 1024)
    tk = _attn_kv_chunk(seq, tm, tq)

    c96, s96, c96t, s96t = _rope_tables(seq)
    cc, sc, m1a, m1b, m2 = _dft_tables(seq)
    row = lambda v: v.reshape(1, -1)
    d_mla = N_HEADS * V_DIM

    h = x
    for l in range(depth):
        wi = w_in[l]
        splits = np.cumsum([Q_LORA, KV_LORA, QK_ROPE, D_CONV, D_CONV, D_CONV])
        w_cq, w_ckv, w_kpe, w_b, w_c, w_h, w_f = jnp.split(wi, splits, axis=-1)
        win = jnp.concatenate([
            w_cq, w_ckv, w_b, w_c, w_h, w_f,
            _pad_cols(w_kpe, QK_NOPE, 128), _pad_cols(_swap_halves(w_kpe), QK_NOPE, 128)], axis=-1).astype(bf16)
        wq = jnp.transpose(w_uq[l], (1, 0, 2))
        wqs = _pad_cols(_swap_halves(wq[..., QK_NOPE:]), QK_NOPE, D_QK)
        wqt = jnp.transpose(wq, (0, 2, 1)).reshape(N_HEADS * D_QK, Q_LORA)
        wqst = jnp.transpose(wqs, (0, 2, 1)).reshape(N_HEADS * D_QK, Q_LORA)
        wkv = jnp.transpose(w_ukv[l], (1, 0, 2))
        wk = _pad_cols(wkv[..., :QK_NOPE], 0, K_LANES)
        wvt = _pad_cols(wkv[..., QK_NOPE:], 0, V_ROWS)
        wvt = jnp.transpose(wvt, (0, 2, 1)).reshape(N_HEADS * V_ROWS, KV_LORA)
        outs = _inproj(h, row(ln_in_g), row(ln_in_b), win, row(g_q[l]), row(g_kv[l]), wqt.astype(bf16),
                       wqst.astype(bf16), wk.astype(bf16), wvt.astype(bf16), c96, s96, c96t, s96t, cc, sc,
                       ln_input=(l == 0), tm=tm, tk=tk)
        if l == 0:
            h, outs = outs[0], outs[1:]
        qt, k, vt, u, bgate, ab = outs
        o_mla = _attention(qt, k, vt, tq=tq)
        o_four = _fourier(ab, m1a, m1b, m2)

        wo = w_out[l].astype(bf16)
        wom = wo[:d_mla]
        woc = wo[d_mla:d_mla + D_CONV]
        wof = wo[d_mla + D_CONV:]
        wr = _pad_cols(jnp.concatenate([w_group[l], w_router[l].reshape(d, N_EXPERTS)], axis=-1), 0, ROUTER_LANES)
        br = _pad_cols(jnp.concatenate([b_group[l], b_router[l].reshape(N_EXPERTS)])[None, :], 0, ROUTER_LANES)
        wrh = wr.astype(bf16)
        wrl = (wr - wrh.astype(jnp.float32)).astype(bf16)
        h1, comb, cnt = _outproj(o_mla, u, bgate, o_four, h, wom, woc, wof, w_conv[l], row(ln1_g[l]), row(ln1_b[l]),
                            wrh, wrl, br, alpha=alpha, tm=tm)

        wgu, wd = _moe_weights(w_gate, w_up, w_down, l)
        offs = _group_offsets(cnt, MOE_TILE // tm)
        h = _moe(h1.reshape(bsz * seq, d), comb.reshape(bsz * seq, ROUTER_LANES), offs, wgu, wd,
                 row(ln2_g[l]), row(ln2_b[l]), alpha=alpha).reshape(bsz, seq, d)
    return h
```

```python
import functools
import math

import numpy as np
import jax
import jax.numpy as jnp
from jax import lax
from jax.experimental import pallas as pl
from jax.experimental.pallas import tpu as pltpu

N_HEADS = 8
QK_NOPE = 64
QK_ROPE = 32
D_QK = QK_NOPE + QK_ROPE
K_LANES = 128
V_DIM = 64
V_ROWS = 80
Q_LORA = 384
KV_LORA = 256
D_CONV = 256
D_FOURIER = 256
FOURIER_GROUP = 64
N_GROUPS = 4
EXPERTS_PER_GROUP = 8
N_EXPERTS = N_GROUPS * EXPERTS_PER_GROUP
D_EXPERT = 256
ROPE_THETA = 10000.0
LN_EPS = 1e-5
RMS_EPS = 1e-6
DFT_N1 = 128
ROUTER_LANES = 128
GID_LANE = N_GROUPS + N_EXPERTS
MOE_TILE = 1024
MOE_BLK = 256
SUBLANES = 8
V7X_VMEM_BYTES = 64 * 1024 * 1024
VMEM_LIMIT = V7X_VMEM_BYTES - 8 * 1024 * 1024
ATTN_SCORE_BUF_BYTES = 8 * 1024 * 1024

_O_CQ = 0
_O_CKV = _O_CQ + Q_LORA
_O_B = _O_CKV + KV_LORA
_O_C = _O_B + D_CONV
_O_H = _O_C + D_CONV
_O_F = _O_H + D_CONV
_O_KPE = _O_F + D_FOURIER
_O_KPES = _O_KPE + K_LANES
_W_IN_COLS = _O_KPES + K_LANES


def _cparams(sem):
    return pltpu.CompilerParams(dimension_semantics=sem, vmem_limit_bytes=VMEM_LIMIT)


def _layer_norm(x, g, b):
    mu = jnp.mean(x, axis=-1, keepdims=True)
    xc = x - mu
    var = jnp.mean(xc * xc, axis=-1, keepdims=True)
    return xc * lax.rsqrt(var + LN_EPS) * g + b


def _rms_norm(x, g):
    return x * lax.rsqrt(jnp.mean(x * x, axis=-1, keepdims=True) + RMS_EPS) * g


def _bdot(a, b):
    return jnp.dot(a, b, preferred_element_type=jnp.float32)


def _dot_nt(a, b):
    return lax.dot_general(a, b, (((1,), (1,)), ((), ())), preferred_element_type=jnp.float32)


def _inproj_kernel(x_ref, lng_ref, lnb_ref, win_ref, gq_ref, gkv_ref, wqt_ref, wqst_ref, wk_ref, wvt_ref,
                   c96_ref, s96_ref, c96t_ref, s96t_ref, cc_ref, sc_ref, *out_refs, ln_input, q_scale):
    if ln_input:
        h_ref, qt_ref, k_ref, vt_ref, u_ref, b_ref, ab_ref = out_refs
        h = _layer_norm(x_ref[0], lng_ref[...], lnb_ref[...])
        h_ref[0] = h
    else:
        qt_ref, k_ref, vt_ref, u_ref, b_ref, ab_ref = out_refs
        h = x_ref[0]
    z = _bdot(h.astype(jnp.bfloat16), win_ref[...])
    c96 = c96_ref[...]
    s96 = s96_ref[...]
    c96t = c96t_ref[...]
    s96t = s96t_ref[...]
    cqn = _rms_norm(z[:, _O_CQ:_O_CQ + Q_LORA], gq_ref[...]).astype(jnp.bfloat16)
    ckvn = _rms_norm(z[:, _O_CKV:_O_CKV + KV_LORA], gkv_ref[...]).astype(jnp.bfloat16)
    kpe = z[:, _O_KPE:_O_KPE + K_LANES] * c96 + z[:, _O_KPES:_O_KPES + K_LANES] * s96
    qt_all = _dot_nt(wqt_ref[...], cqn)
    qst_all = _dot_nt(wqst_ref[...], cqn)
    vt_all = _dot_nt(wvt_ref[...], ckvn)
    vrow = lax.broadcasted_iota(jnp.int32, (V_ROWS, vt_all.shape[1]), 0)
    for hd in range(N_HEADS):
        qt = (qt_all[hd * D_QK:(hd + 1) * D_QK] * c96t + qst_all[hd * D_QK:(hd + 1) * D_QK] * s96t) * q_scale
        qt_ref[0, hd, 0] = qt.astype(qt_ref.dtype)
        k_ref[0, hd] = (_bdot(ckvn, wk_ref[hd]) + kpe).astype(k_ref.dtype)
        vt = jnp.where(vrow == V_DIM, 1.0, vt_all[hd * V_ROWS:(hd + 1) * V_ROWS])
        vt_ref[0, hd, 0] = vt.astype(vt_ref.dtype)
    u_ref[0] = z[:, _O_C:_O_C + D_CONV] * z[:, _O_H:_O_H + D_CONV]
    b_ref[0] = z[:, _O_B:_O_B + D_CONV]
    fb = z[:, _O_F:_O_F + D_FOURIER].astype(jnp.bfloat16)
    ab_ref[0, :, :D_FOURIER] = _bdot(fb, cc_ref[...]).astype(ab_ref.dtype)
    ab_ref[0, :, D_FOURIER:] = _bdot(fb, sc_ref[...]).astype(ab_ref.dtype)


def _inproj(x, lng, lnb, win, gq, gkv, wqt, wqst, wk, wvt, c96, s96, c96t, s96t, cc, sc, *, ln_input, tm, tq, tk):
    bsz, seq, d = x.shape
    q_scale = (D_QK ** -0.5) * math.log2(math.e)
    grid = (bsz, seq // tm)
    const2 = lambda b, i: (0, 0)
    const3 = lambda b, i: (0, 0, 0)
    in_specs = [
        pl.BlockSpec((1, tm, d), lambda b, i: (b, i, 0)),
        pl.BlockSpec((1, d), const2),
        pl.BlockSpec((1, d), const2),
        pl.BlockSpec(win.shape, const2),
        pl.BlockSpec((1, Q_LORA), const2),
        pl.BlockSpec((1, KV_LORA), const2),
        pl.BlockSpec(wqt.shape, const2),
        pl.BlockSpec(wqst.shape, const2),
        pl.BlockSpec(wk.shape, const3),
        pl.BlockSpec(wvt.shape, const2),
        pl.BlockSpec((tm, K_LANES), lambda b, i: (i, 0)),
        pl.BlockSpec((tm, K_LANES), lambda b, i: (i, 0)),
        pl.BlockSpec((D_QK, tm), lambda b, i: (0, i)),
        pl.BlockSpec((D_QK, tm), lambda b, i: (0, i)),
        pl.BlockSpec(cc.shape, const2),
        pl.BlockSpec(sc.shape, const2),
    ]
    tok_spec = lambda w: pl.BlockSpec((1, tm, w), lambda b, i: (b, i, 0))
    out_shape = [
        jax.ShapeDtypeStruct((bsz, N_HEADS, seq // tq, D_QK, tq), jnp.bfloat16),
        jax.ShapeDtypeStruct((bsz, N_HEADS, seq, K_LANES), jnp.bfloat16),
        jax.ShapeDtypeStruct((bsz, N_HEADS, seq // tk, V_ROWS, tk), jnp.bfloat16),
        jax.ShapeDtypeStruct((bsz, seq, D_CONV), jnp.float32),
        jax.ShapeDtypeStruct((bsz, seq, D_CONV), jnp.float32),
        jax.ShapeDtypeStruct((bsz, seq, 2 * D_FOURIER), jnp.bfloat16),
    ]
    out_specs = [pl.BlockSpec((1, N_HEADS, 1, D_QK, tm), lambda b, i: (b, 0, i // (tq // tm), 0, i % (tq // tm))),
                 pl.BlockSpec((1, N_HEADS, tm, K_LANES), lambda b, i: (b, 0, i, 0)),
                 pl.BlockSpec((1, N_HEADS, 1, V_ROWS, tm), lambda b, i: (b, 0, i // (tk // tm), 0, i % (tk // tm))),
                 tok_spec(D_CONV), tok_spec(D_CONV), tok_spec(2 * D_FOURIER)]
    if ln_input:
        out_shape = [jax.ShapeDtypeStruct((bsz, seq, d), jnp.float32)] + out_shape
        out_specs = [tok_spec(d)] + out_specs
    return pl.pallas_call(
        functools.partial(_inproj_kernel, ln_input=ln_input, q_scale=q_scale),
        grid=grid, in_specs=in_specs, out_specs=out_specs, out_shape=out_shape,
        compiler_params=_cparams(("parallel", "parallel")), name="inproj",
    )(x, lng, lnb, win, gq, gkv, wqt, wqst, wk, wvt, c96, s96, c96t, s96t, cc, sc)


def _attn_kernel(qt_ref, k_ref, vt_ref, o_ref, s0_ref, s1_ref, c0_ref, c1_ref, m_ref, acc_ref, *, tk, n_kv, n_q):
    n_t = n_q * n_kv

    def scores(t, s_ref, c_ref):
        off = pl.multiple_of((t % n_kv) * tk, tk)
        s = _bdot(k_ref[0, 0, pl.ds(off, tk), :D_QK], qt_ref[0, 0, t // n_kv])
        s_ref[...] = s
        c_ref[...] = jnp.max(s, axis=0, keepdims=True)

    def update(t, s_ref, c_ref):
        j = t % n_kv
        m_old = jnp.where(j == 0, -jnp.inf, m_ref[...])
        m_new = jnp.maximum(m_old, c_ref[...])
        p = jnp.exp2(s_ref[...] - m_new).astype(jnp.bfloat16)
        acc_ref[...] = jnp.exp2(m_old - m_new) * acc_ref[...] + _bdot(vt_ref[0, 0, j], p)
        m_ref[...] = m_new

    def emit(t):
        acc = acc_ref[...]
        o_ref[0, 0, t // n_kv] = (acc[:V_DIM] / acc[V_DIM:V_DIM + 1]).astype(o_ref.dtype)

    m_ref[...] = jnp.full(m_ref.shape, -jnp.inf, jnp.float32)
    acc_ref[...] = jnp.zeros(acc_ref.shape, jnp.float32)
    scores(0, s0_ref, c0_ref)

    def pair(i, carry):
        t = 2 * i
        scores(t + 1, s1_ref, c1_ref)
        update(t, s0_ref, c0_ref)
        scores(t + 2, s0_ref, c0_ref)
        update(t + 1, s1_ref, c1_ref)

        @pl.when((t + 1) % n_kv == n_kv - 1)
        def _():
            emit(t + 1)

        return carry

    lax.fori_loop(0, n_t // 2 - 1, pair, 0)
    scores(n_t - 1, s1_ref, c1_ref)
    update(n_t - 2, s0_ref, c0_ref)
    update(n_t - 1, s1_ref, c1_ref)
    emit(n_t - 1)


def _attention(qt, k, vt):
    bsz, nh, seq, _ = k.shape
    n_q, _, tq = qt.shape[2:]
    n_kv, _, tk = vt.shape[2:]
    assert n_kv % 2 == 0 and n_kv * tk == seq and n_q * tq == seq
    return pl.pallas_call(
        functools.partial(_attn_kernel, tk=tk, n_kv=n_kv, n_q=n_q),
        grid=(bsz, nh),
        in_specs=[
            pl.BlockSpec((1, 1, n_q, D_QK, tq), lambda b, h: (b, h, 0, 0, 0)),
            pl.BlockSpec((1, 1, seq, K_LANES), lambda b, h: (b, h, 0, 0)),
            pl.BlockSpec((1, 1, n_kv, V_ROWS, tk), lambda b, h: (b, h, 0, 0, 0)),
        ],
        out_specs=pl.BlockSpec((1, 1, n_q, V_DIM, tq), lambda b, h: (b, h, 0, 0, 0)),
        out_shape=jax.ShapeDtypeStruct((bsz, nh, n_q, V_DIM, tq), jnp.bfloat16),
        scratch_shapes=[pltpu.VMEM((tk, tq), jnp.float32), pltpu.VMEM((tk, tq), jnp.float32),
                        pltpu.VMEM((1, tq), jnp.float32), pltpu.VMEM((1, tq), jnp.float32),
                        pltpu.VMEM((1, tq), jnp.float32), pltpu.VMEM((V_ROWS, tq), jnp.float32)],
        compiler_params=_cparams(("parallel", "parallel")), name="attention",
    )(qt, k, vt)


def _attn_kv_chunk(seq, tm, tq):
    tk = tm
    while seq % (4 * tk) == 0 and 2 * tk * tq * 4 <= ATTN_SCORE_BUF_BYTES:
        tk *= 2
    return tk


def _fourier_a_kernel(ab_ref, m1a_ref, m1b_ref, y_ref, *, n2c):
    for j in range(n2c):
        a = ab_ref[0, :, j * 2 * D_FOURIER: j * 2 * D_FOURIER + D_FOURIER]
        b = ab_ref[0, :, j * 2 * D_FOURIER + D_FOURIER: (j + 1) * 2 * D_FOURIER]
        y = _bdot(m1a_ref[...], a) + _bdot(m1b_ref[...], b)
        y_ref[0, :, j * D_FOURIER:(j + 1) * D_FOURIER] = y.astype(y_ref.dtype)


def _fourier_b_kernel(yr_ref, yi_ref, m2_ref, o_ref, *, k1c, n2):
    for j in range(k1c):
        m2 = m2_ref[j]
        z = _bdot(m2[:, :n2], yr_ref[0, j]) + _bdot(m2[:, n2:], yi_ref[0, j])
        o_ref[0, :, j, :] = z.astype(o_ref.dtype)


def _fourier(ab, m1a, m1b, m2):
    bsz, seq, _ = ab.shape
    n1 = DFT_N1
    n2 = seq // n1
    n2c = min(n2, 16)
    k1c = 8
    abv = ab.reshape(bsz, n1, n2 * 2 * D_FOURIER)
    y = pl.pallas_call(
        functools.partial(_fourier_a_kernel, n2c=n2c),
        grid=(bsz, n2 // n2c),
        in_specs=[
            pl.BlockSpec((1, n1, n2c * 2 * D_FOURIER), lambda b, c: (b, 0, c)),
            pl.BlockSpec(m1a.shape, lambda b, c: (0, 0)),
            pl.BlockSpec(m1b.shape, lambda b, c: (0, 0)),
        ],
        out_specs=pl.BlockSpec((1, 2 * n1, n2c * D_FOURIER), lambda b, c: (b, 0, c)),
        out_shape=jax.ShapeDtypeStruct((bsz, 2 * n1, n2 * D_FOURIER), jnp.bfloat16),
        compiler_params=_cparams(("parallel", "parallel")), name="fourier_a",
    )(abv, m1a, m1b)
    yv = y.reshape(bsz, 2 * n1, n2, D_FOURIER)
    nblk = n1 // k1c
    out = pl.pallas_call(
        functools.partial(_fourier_b_kernel, k1c=k1c, n2=n2),
        grid=(bsz, nblk),
        in_specs=[
            pl.BlockSpec((1, k1c, n2, D_FOURIER), lambda b, c: (b, c, 0, 0)),
            pl.BlockSpec((1, k1c, n2, D_FOURIER), lambda b, c: (b, nblk + c, 0, 0)),
            pl.BlockSpec((k1c, n2, 2 * n2), lambda b, c: (c, 0, 0)),
        ],
        out_specs=pl.BlockSpec((1, n2, k1c, D_FOURIER), lambda b, c: (b, 0, c, 0)),
        out_shape=jax.ShapeDtypeStruct((bsz, n2, n1, D_FOURIER), jnp.float32),
        compiler_params=_cparams(("parallel", "parallel")), name="fourier_b",
    )(yv, yv, m2)
    return out.reshape(bsz, seq, D_FOURIER)


def _dft_tables(seq):
    n1 = DFT_N1
    n2 = seq // n1
    c = np.arange(FOURIER_GROUP)
    ang = 2.0 * np.pi * np.outer(c, c) / FOURIER_GROUP
    eye = np.eye(D_FOURIER // FOURIER_GROUP)
    cc = np.kron(eye, np.cos(ang))
    sc = np.kron(eye, np.sin(ang))
    i1 = np.arange(n1)
    a1 = 2.0 * np.pi * np.outer(i1, i1) / n1
    c1, s1 = np.cos(a1), np.sin(a1)
    m1a = np.concatenate([c1, -s1], axis=0)
    m1b = np.concatenate([-s1, -c1], axis=0)
    k = i1[:, None] + n1 * np.arange(n2)[None, :]
    phi = 2.0 * np.pi * (k[:, :, None] * np.arange(n2)[None, None, :] % seq) / seq
    norm = 1.0 / math.sqrt(seq * FOURIER_GROUP)
    m2 = np.concatenate([np.cos(phi), np.sin(phi)], axis=-1) * norm
    bf = lambda t: jnp.asarray(t, jnp.float32).astype(jnp.bfloat16)
    return bf(cc), bf(sc), bf(m1a), bf(m1b), bf(m2)


def _outproj_kernel(o_ref, u_ref, up_ref, un_ref, b_ref, f_ref, h_ref, wom_ref, woc_ref, wof_ref, wconv_ref,
                    g_ref, beta_ref, wrh_ref, wrl_ref, br_ref, h1_ref, comb_ref, cnt_ref, *, alpha):
    i = pl.program_id(1)
    last = pl.num_programs(1) - 1
    tm = u_ref.shape[1]
    u = u_ref[0]
    up = up_ref[0][SUBLANES - 1:, :] * (i > 0).astype(jnp.float32)
    un = un_ref[0][0:1, :] * (i < last).astype(jnp.float32)
    row = lax.broadcasted_iota(jnp.int32, u.shape, 0)
    u_m1 = jnp.where(row == 0, up, pltpu.roll(u, 1, axis=0))
    u_p1 = jnp.where(row == tm - 1, un, pltpu.roll(u, tm - 1, axis=0))
    wc = wconv_ref[...]
    oconv = b_ref[0] * (u_m1 * wc[0:1, :] + u * wc[1:2, :] + u_p1 * wc[2:3, :])
    mix = _bdot(oconv.astype(jnp.bfloat16), woc_ref[...]) + _bdot(f_ref[0].astype(jnp.bfloat16), wof_ref[...])
    o_t = o_ref[0, :, 0].reshape(N_HEADS * V_DIM, o_ref.shape[-1])
    mix = mix + lax.dot_general(o_t, wom_ref[...], (((0,), (0,)), ((), ())),
                                preferred_element_type=jnp.float32)
    h1 = _layer_norm(alpha * h_ref[0] + mix, g_ref[...], beta_ref[...])
    h1_ref[0] = h1

    h_hi = h1.astype(jnp.bfloat16)
    h_lo = (h1 - h_hi.astype(jnp.float32)).astype(jnp.bfloat16)
    logits = (_bdot(h_hi, wrh_ref[...]) + (_bdot(h_hi, wrl_ref[...]) + _bdot(h_lo, wrh_ref[...]))) + br_ref[...]
    lane = lax.broadcasted_iota(jnp.int32, logits.shape, 1)
    neg = jnp.float32(-jnp.inf)
    big = jnp.int32(1 << 20)
    is_g = lane < N_GROUPS
    gl = jnp.where(is_g, logits, neg)
    gmax = jnp.max(gl, axis=-1, keepdims=True)
    p_group = 1.0 / jnp.sum(jnp.exp(gl - gmax), axis=-1, keepdims=True)
    g_sel = jnp.min(jnp.where(is_g & (logits == gmax), lane, big), axis=-1, keepdims=True)
    e_lo = N_GROUPS + g_sel * EXPERTS_PER_GROUP
    is_e = (lane >= e_lo) & (lane < e_lo + EXPERTS_PER_GROUP)
    l1 = jnp.max(jnp.where(is_e, logits, neg), axis=-1, keepdims=True)
    i1 = jnp.min(jnp.where(is_e & (logits == l1), lane, big), axis=-1, keepdims=True)
    is_e2 = is_e & (lane != i1)
    l2 = jnp.max(jnp.where(is_e2, logits, neg), axis=-1, keepdims=True)
    i2 = jnp.min(jnp.where(is_e2 & (logits == l2), lane, big), axis=-1, keepdims=True)
    r = jnp.exp(l2 - l1)
    w1 = 1.0 / (1.0 + r)
    w2 = r * w1
    comb = jnp.where(lane == i1, w1, jnp.where(lane == i2, w2, 0.0)) * p_group
    comb_ref[0] = jnp.where(lane == GID_LANE, g_sel.astype(jnp.float32), comb)
    cnt_ref[0, 0] = jnp.sum(jnp.where(lane == g_sel, 1.0, 0.0), axis=0, keepdims=True)


def _outproj(o, u, bgate, ofour, h, wom, woc, wof, wconv, g, beta, wrh, wrl, br, *, alpha, tm):
    bsz, seq, d = h.shape
    tq = o.shape[-1]
    n_row_blocks = seq // SUBLANES
    rows_per_tile = tm // SUBLANES
    const2 = lambda b, i: (0, 0)
    const3 = lambda b, i: (0, 0, 0)
    tok = lambda w: pl.BlockSpec((1, tm, w), lambda b, i: (b, i, 0))
    return pl.pallas_call(
        functools.partial(_outproj_kernel, alpha=alpha),
        grid=(bsz, seq // tm),
        in_specs=[
            pl.BlockSpec((1, N_HEADS, 1, V_DIM, tm), lambda b, i: (b, 0, i // (tq // tm), 0, i % (tq // tm))),
            tok(D_CONV),
            pl.BlockSpec((1, SUBLANES, D_CONV), lambda b, i: (b, jnp.maximum(i * rows_per_tile - 1, 0), 0)),
            pl.BlockSpec((1, SUBLANES, D_CONV),
                         lambda b, i: (b, jnp.minimum((i + 1) * rows_per_tile, n_row_blocks - 1), 0)),
            tok(D_CONV),
            tok(D_FOURIER),
            tok(d),
            pl.BlockSpec(wom.shape, const2),
            pl.BlockSpec(woc.shape, const2),
            pl.BlockSpec(wof.shape, const2),
            pl.BlockSpec(wconv.shape, const2),
            pl.BlockSpec((1, d), const2),
            pl.BlockSpec((1, d), const2),
            pl.BlockSpec(wrh.shape, const2),
            pl.BlockSpec(wrl.shape, const2),
            pl.BlockSpec((1, ROUTER_LANES), const2),
        ],
        out_specs=[tok(d), tok(ROUTER_LANES), pl.BlockSpec((1, 1, 1, ROUTER_LANES), lambda b, i: (b, i, 0, 0))],
        out_shape=[jax.ShapeDtypeStruct((bsz, seq, d), jnp.float32),
                   jax.ShapeDtypeStruct((bsz, seq, ROUTER_LANES), jnp.float32),
                   jax.ShapeDtypeStruct((bsz, seq // tm, 1, ROUTER_LANES), jnp.float32)],
        compiler_params=_cparams(("parallel", "parallel")), name="outproj",
    )(o, u, u, u, bgate, ofour, h, wom, woc, wof, wconv, g, beta, wrh, wrl, br)


def _dot_tn(a, b):
    return lax.dot_general(a, b, (((0,), (0,)), ((), ())), preferred_element_type=jnp.float32)


def _moe_kernel(offs_ref, h_ref, comb_ref, wgu_ref, wd_ref, g_ref, beta_ref, o_ref, pt_ref, xs_ref, cw_ref, ys_ref,
                *, alpha):
    i = pl.program_id(0)
    s = pl.program_id(1)
    n_blk = MOE_TILE // MOE_BLK
    e_step = EXPERTS_PER_GROUP
    f32, bf16 = jnp.float32, jnp.bfloat16

    @pl.when(s == 0)
    def _():
        comb = comb_ref[...]
        lane = lax.broadcasted_iota(jnp.int32, comb.shape, 1)
        in_grp = lane.astype(f32) == comb[:, GID_LANE:GID_LANE + 1]
        row = lax.broadcasted_iota(jnp.int32, (MOE_TILE, MOE_TILE), 0)
        col = lax.broadcasted_iota(jnp.int32, (MOE_TILE, MOE_TILE), 1)
        lower = jnp.where(row >= col, 1.0, 0.0).astype(bf16)
        rank = _bdot(lower, jnp.where(in_grp, 1.0, 0.0).astype(bf16))
        lane1 = lax.broadcasted_iota(jnp.int32, (1, ROUTER_LANES), 1)
        start = jnp.zeros((1, ROUTER_LANES), f32)
        for gg in range(N_GROUPS):
            start = jnp.where(lane1 == gg, offs_ref[i, gg].astype(f32), start)
        pos = jnp.sum(jnp.where(in_grp, start + rank - 1.0, 0.0), axis=-1, keepdims=True)
        pt = jnp.where(col.astype(f32) == pos, 1.0, 0.0).astype(bf16)
        pt_ref[...] = pt
        c1 = comb.astype(bf16)
        r1 = comb - c1.astype(f32)
        c2 = r1.astype(bf16)
        c3 = (r1 - c2.astype(f32)).astype(bf16)
        d = h_ref.shape[1]
        moved = _dot_tn(jnp.concatenate([h_ref[...].astype(bf16), c1, c2, c3], axis=1), pt)
        xs = moved[:d]
        cw = (moved[d:d + ROUTER_LANES] + moved[d + ROUTER_LANES:d + 2 * ROUTER_LANES]
              + moved[d + 2 * ROUTER_LANES:])
        for bb in range(n_blk):
            xs_ref[bb] = xs[:, bb * MOE_BLK:(bb + 1) * MOE_BLK].astype(bf16)
            cw_ref[bb] = cw[:, bb * MOE_BLK:(bb + 1) * MOE_BLK]
        ys_ref[...] = jnp.zeros_like(ys_ref)

    lo = offs_ref[i, s]
    hi = offs_ref[i, s + 1]
    b_lo = lo // MOE_BLK
    b_hi = jnp.where(hi > lo, (hi + MOE_BLK - 1) // MOE_BLK, b_lo)
    cw_row = N_GROUPS + s * e_step

    def block(b, carry):
        au = _bdot(wgu_ref[0], xs_ref[b])
        hid = []
        for e in range(e_step):
            a = au[2 * e * D_EXPERT:(2 * e + 1) * D_EXPERT]
            up = au[(2 * e + 1) * D_EXPERT:(2 * e + 2) * D_EXPERT]
            w = cw_ref[b, pl.ds(cw_row + e, 1), :]
            hid.append((a * (1.0 / (1.0 + jnp.exp(-a))) * up * w).astype(bf16))
        ys_ref[b] += _bdot(wd_ref[0], jnp.concatenate(hid, axis=0))
        return carry

    lax.fori_loop(b_lo, b_hi, block, 0)

    @pl.when(s == pl.num_programs(1) - 1)
    def _():
        ys = jnp.concatenate([ys_ref[bb] for bb in range(n_blk)], axis=1)
        y_hi = ys.astype(bf16)
        y_lo = (ys - y_hi.astype(f32)).astype(bf16)
        pt = pt_ref[...]
        y = _dot_nt(pt, y_hi) + _dot_nt(pt, y_lo)
        o_ref[...] = _layer_norm(alpha * h_ref[...] + y, g_ref[...], beta_ref[...])


def _moe(h, comb, offs, wgu, wd, g, beta, *, alpha):
    n, d = h.shape
    const2 = lambda i, s, offs: (0, 0)
    grid_spec = pltpu.PrefetchScalarGridSpec(
        num_scalar_prefetch=1,
        grid=(n // MOE_TILE, N_GROUPS),
        in_specs=[
            pl.BlockSpec((MOE_TILE, d), lambda i, s, offs: (i, 0), pipeline_mode=pl.Buffered(1)),
            pl.BlockSpec((MOE_TILE, ROUTER_LANES), lambda i, s, offs: (i, 0), pipeline_mode=pl.Buffered(1)),
            pl.BlockSpec((1,) + wgu.shape[1:], lambda i, s, offs: (s, 0, 0)),
            pl.BlockSpec((1,) + wd.shape[1:], lambda i, s, offs: (s, 0, 0)),
            pl.BlockSpec((1, d), const2),
            pl.BlockSpec((1, d), const2),
        ],
        out_specs=pl.BlockSpec((MOE_TILE, d), lambda i, s, offs: (i, 0)),
        scratch_shapes=[
            pltpu.VMEM((MOE_TILE, MOE_TILE), jnp.bfloat16),
            pltpu.VMEM((MOE_TILE // MOE_BLK, d, MOE_BLK), jnp.bfloat16),
            pltpu.VMEM((MOE_TILE // MOE_BLK, ROUTER_LANES, MOE_BLK), jnp.float32),
            pltpu.VMEM((MOE_TILE // MOE_BLK, d, MOE_BLK), jnp.float32),
        ],
    )
    return pl.pallas_call(
        functools.partial(_moe_kernel, alpha=alpha),
        grid_spec=grid_spec,
        out_shape=jax.ShapeDtypeStruct((n, d), jnp.float32),
        compiler_params=_cparams(("parallel", "arbitrary")), name="moe",
    )(offs, h, comb, wgu, wd, g, beta)


def _moe_weights_kernel(wg_ref, wu_ref, wd_ref, wgu_ref, wdt_ref):
    wgu_ref[0, 0] = wg_ref[0, 0, 0].T.astype(wgu_ref.dtype)
    wgu_ref[0, 1] = wu_ref[0, 0, 0].T.astype(wgu_ref.dtype)
    wdt_ref[0] = wd_ref[0, 0, 0].T.astype(wdt_ref.dtype)


def _moe_weights(w_gate, w_up, w_down, layer):
    _, ng, ne, d, f = w_gate.shape
    w_spec = lambda a, b: pl.BlockSpec((1, 1, 1, a, b), lambda g, e: (layer, g, e, 0, 0))
    wgu, wdt = pl.pallas_call(
        _moe_weights_kernel,
        grid=(ng, ne),
        in_specs=[w_spec(d, f), w_spec(d, f), w_spec(f, d)],
        out_specs=[pl.BlockSpec((1, 2, f, d), lambda g, e: (g * ne + e, 0, 0, 0)),
                   pl.BlockSpec((1, d, f), lambda g, e: (g, 0, e))],
        out_shape=[jax.ShapeDtypeStruct((ng * ne, 2, f, d), jnp.bfloat16),
                   jax.ShapeDtypeStruct((ng, d, ne * f), jnp.bfloat16)],
        compiler_params=_cparams(("parallel", "parallel")), name="moe_weights",
    )(w_gate, w_up, w_down)
    return wgu.reshape(ng, ne * 2 * f, d), wdt


def _group_offsets(cnt, tiles_per_moe_tile):
    c = cnt.reshape(-1, tiles_per_moe_tile, ROUTER_LANES)[:, :, :N_GROUPS].sum(axis=1)
    ends = jnp.cumsum(c, axis=-1)
    offs = jnp.concatenate([jnp.zeros_like(ends[:, :1]), ends], axis=-1)
    return offs.astype(jnp.int32)


def _rope_tables(seq):
    inv_freq = 1.0 / (ROPE_THETA ** (jnp.arange(0, QK_ROPE, 2, dtype=jnp.float32) / QK_ROPE))
    ang = jnp.arange(seq, dtype=jnp.float32)[:, None] * inv_freq[None, :]
    cos, sin = jnp.cos(ang), jnp.sin(ang)
    c96 = jnp.concatenate([jnp.ones((seq, QK_NOPE), jnp.float32), cos, cos], axis=-1)
    s96 = jnp.concatenate([jnp.zeros((seq, QK_NOPE), jnp.float32), -sin, sin], axis=-1)
    return _pad_cols(c96, 0, K_LANES), _pad_cols(s96, 0, K_LANES), c96.T, s96.T


def _swap_halves(w):
    half = w.shape[-1] // 2
    return jnp.concatenate([w[..., half:], w[..., :half]], axis=-1)


def _pad_cols(w, left, total):
    return jnp.pad(w, [(0, 0)] * (w.ndim - 1) + [(left, total - left - w.shape[-1])])


def _pick_tile(n, want):
    t = min(n, want)
    while n % t:
        t //= 2
    return t


def kernel(x, ln_in_g, ln_in_b, w_in, g_q, g_kv, w_uq, w_ukv, w_conv, w_out, ln1_g, ln1_b, w_group, b_group,
           w_router, b_router, w_gate, w_up, w_down, ln2_g, ln2_b):
    bsz, seq, d = x.shape
    depth = w_in.shape[0]
    alpha = (2.0 * depth) ** 0.25
    bf16 = jnp.bfloat16
    tm = _pick_tile(seq, 512)
    tq = _pick_tile(seq, 1024)
    tk = _attn_kv_chunk(seq, tm, tq)

    c96, s96, c96t, s96t = _rope_tables(seq)
    cc, sc, m1a, m1b, m2 = _dft_tables(seq)
    row = lambda v: v.reshape(1, -1)
    d_mla = N_HEADS * V_DIM

    h = x
    for l in range(depth):
        wi = w_in[l]
        splits = np.cumsum([Q_LORA, KV_LORA, QK_ROPE, D_CONV, D_CONV, D_CONV])
        w_cq, w_ckv, w_kpe, w_b, w_c, w_h, w_f = jnp.split(wi, splits, axis=-1)
        win = jnp.concatenate([
            w_cq, w_ckv, w_b, w_c, w_h, w_f,
            _pad_cols(w_kpe, QK_NOPE, K_LANES), _pad_cols(_swap_halves(w_kpe), QK_NOPE, K_LANES)],
            axis=-1).astype(bf16)
        wq = jnp.transpose(w_uq[l], (1, 0, 2))
        wqs = _pad_cols(_swap_halves(wq[..., QK_NOPE:]), QK_NOPE, D_QK)
        wqt = jnp.transpose(wq, (0, 2, 1)).reshape(N_HEADS * D_QK, Q_LORA)
        wqst = jnp.transpose(wqs, (0, 2, 1)).reshape(N_HEADS * D_QK, Q_LORA)
        wkv = jnp.transpose(w_ukv[l], (1, 0, 2))
        wk = _pad_cols(wkv[..., :QK_NOPE], 0, K_LANES)
        wvt = _pad_cols(wkv[..., QK_NOPE:], 0, V_ROWS)
        wvt = jnp.transpose(wvt, (0, 2, 1)).reshape(N_HEADS * V_ROWS, KV_LORA)
        outs = _inproj(h, row(ln_in_g), row(ln_in_b), win, row(g_q[l]), row(g_kv[l]), wqt.astype(bf16),
                       wqst.astype(bf16), wk.astype(bf16), wvt.astype(bf16), c96, s96, c96t, s96t, cc, sc,
                       ln_input=(l == 0), tm=tm, tq=tq, tk=tk)
        if l == 0:
            h, outs = outs[0], outs[1:]
        qt, k, vt, u, bgate, ab = outs
        o_mla = _attention(qt, k, vt)
        o_four = _fourier(ab, m1a, m1b, m2)

        wo = w_out[l].astype(bf16)
        wom = wo[:d_mla]
        woc = wo[d_mla:d_mla + D_CONV]
        wof = wo[d_mla + D_CONV:]
        wr = _pad_cols(jnp.concatenate([w_group[l], w_router[l].reshape(d, N_EXPERTS)], axis=-1), 0, ROUTER_LANES)
        br = _pad_cols(jnp.concatenate([b_group[l], b_router[l].reshape(N_EXPERTS)])[None, :], 0, ROUTER_LANES)
        wrh = wr.astype(bf16)
        wrl = (wr - wrh.astype(jnp.float32)).astype(bf16)
        h1, comb, cnt = _outproj(o_mla, u, bgate, o_four, h, wom, woc, wof, w_conv[l], row(ln1_g[l]), row(ln1_b[l]),
                            wrh, wrl, br, alpha=alpha, tm=tm)

        wgu, wd = _moe_weights(w_gate, w_up, w_down, l)
        offs = _group_offsets(cnt, MOE_TILE // tm)
        h = _moe(h1.reshape(bsz * seq, d), comb.reshape(bsz * seq, ROUTER_LANES), offs, wgu, wd,
                 row(ln2_g[l]), row(ln2_b[l]), alpha=alpha).reshape(bsz, seq, d)
    return h
```

```python
import functools
import math

import numpy as np
import jax
import jax.numpy as jnp
from jax import lax
from jax.experimental import pallas as pl
from jax.experimental.pallas import tpu as pltpu

N_HEADS = 8
QK_NOPE = 64
QK_ROPE = 32
D_QK = QK_NOPE + QK_ROPE
K_LANES = 128
V_DIM = 64
V_ROWS = 80
Q_LORA = 384
KV_LORA = 256
D_CONV = 256
D_FOURIER = 256
FOURIER_GROUP = 64
N_GROUPS = 4
EXPERTS_PER_GROUP = 8
N_EXPERTS = N_GROUPS * EXPERTS_PER_GROUP
D_EXPERT = 256
ROPE_THETA = 10000.0
LN_EPS = 1e-5
RMS_EPS = 1e-6
DFT_N1 = 128
ROUTER_LANES = 128
GID_LANE = N_GROUPS + N_EXPERTS
MOE_TILE = 1024
MOE_BLK = 256
SUBLANES = 8
V7X_VMEM_BYTES = 64 * 1024 * 1024
VMEM_LIMIT = V7X_VMEM_BYTES - 8 * 1024 * 1024
ATTN_SCORE_BUF_BYTES = 8 * 1024 * 1024
ATTN_QK_PARTS = 2
MXU_DEPTH = 256

_O_CQ = 0
_O_CKV = _O_CQ + Q_LORA
_O_B = _O_CKV + KV_LORA
_O_C = _O_B + D_CONV
_O_H = _O_C + D_CONV
_O_F = _O_H + D_CONV
_O_KPE = _O_F + D_FOURIER
_O_KPES = _O_KPE + K_LANES
_W_IN_COLS = _O_KPES + K_LANES


def _cparams(sem):
    return pltpu.CompilerParams(dimension_semantics=sem, vmem_limit_bytes=VMEM_LIMIT)


def _layer_norm(x, g, b):
    mu = jnp.mean(x, axis=-1, keepdims=True)
    xc = x - mu
    var = jnp.mean(xc * xc, axis=-1, keepdims=True)
    return xc * lax.rsqrt(var + LN_EPS) * g + b


def _rms_norm(x, g):
    return x * lax.rsqrt(jnp.mean(x * x, axis=-1, keepdims=True) + RMS_EPS) * g


def _bdot(a, b):
    return jnp.dot(a, b, preferred_element_type=jnp.float32)


def _dot_nt(a, b):
    return lax.dot_general(a, b, (((1,), (1,)), ((), ())), preferred_element_type=jnp.float32)


def _inproj_kernel(x_ref, lng_ref, lnb_ref, win_ref, gq_ref, gkv_ref, wqt_ref, wqst_ref, wk_ref, wvt_ref,
                   c96_ref, s96_ref, c96t_ref, s96t_ref, cc_ref, sc_ref, *out_refs, ln_input, q_scale):
    if ln_input:
        h_ref, qt_ref, k_ref, vt_ref, u_ref, b_ref, ab_ref = out_refs
        h = _layer_norm(x_ref[0], lng_ref[...], lnb_ref[...])
        h_ref[0] = h
    else:
        qt_ref, k_ref, vt_ref, u_ref, b_ref, ab_ref = out_refs
        h = x_ref[0]
    z = _bdot(h.astype(jnp.bfloat16), win_ref[...])
    c96 = c96_ref[...]
    s96 = s96_ref[...]
    c96t = c96t_ref[...]
    s96t = s96t_ref[...]
    cqn = _rms_norm(z[:, _O_CQ:_O_CQ + Q_LORA], gq_ref[...]).astype(jnp.bfloat16)
    ckvn = _rms_norm(z[:, _O_CKV:_O_CKV + KV_LORA], gkv_ref[...]).astype(jnp.bfloat16)
    kpe = z[:, _O_KPE:_O_KPE + K_LANES] * c96 + z[:, _O_KPES:_O_KPES + K_LANES] * s96
    qt_all = _dot_nt(wqt_ref[...], cqn)
    qst_all = _dot_nt(wqst_ref[...], cqn)
    vt_all = _dot_nt(wvt_ref[...], ckvn)
    vrow = lax.broadcasted_iota(jnp.int32, (V_ROWS, vt_all.shape[1]), 0)
    for hd in range(N_HEADS):
        qt = (qt_all[hd * D_QK:(hd + 1) * D_QK] * c96t + qst_all[hd * D_QK:(hd + 1) * D_QK] * s96t) * q_scale
        qt_ref[0, hd, 0] = qt.astype(qt_ref.dtype)
        k_ref[0, hd] = (_bdot(ckvn, wk_ref[hd]) + kpe).astype(k_ref.dtype)
        vt = jnp.where(vrow == V_DIM, 1.0, vt_all[hd * V_ROWS:(hd + 1) * V_ROWS])
        vt_ref[0, hd, 0] = vt.astype(vt_ref.dtype)
    u_ref[0] = z[:, _O_C:_O_C + D_CONV] * z[:, _O_H:_O_H + D_CONV]
    b_ref[0] = z[:, _O_B:_O_B + D_CONV]
    fb = z[:, _O_F:_O_F + D_FOURIER].astype(jnp.bfloat16)
    ab_ref[0, :, :D_FOURIER] = _bdot(fb, cc_ref[...]).astype(ab_ref.dtype)
    ab_ref[0, :, D_FOURIER:] = _bdot(fb, sc_ref[...]).astype(ab_ref.dtype)


def _inproj(x, lng, lnb, win, gq, gkv, wqt, wqst, wk, wvt, c96, s96, c96t, s96t, cc, sc, *, ln_input, tm, tq, tk):
    bsz, seq, d = x.shape
    q_scale = (D_QK ** -0.5) * math.log2(math.e)
    grid = (bsz, seq // tm)
    const2 = lambda b, i: (0, 0)
    const3 = lambda b, i: (0, 0, 0)
    in_specs = [
        pl.BlockSpec((1, tm, d), lambda b, i: (b, i, 0)),
        pl.BlockSpec((1, d), const2),
        pl.BlockSpec((1, d), const2),
        pl.BlockSpec(win.shape, const2),
        pl.BlockSpec((1, Q_LORA), const2),
        pl.BlockSpec((1, KV_LORA), const2),
        pl.BlockSpec(wqt.shape, const2),
        pl.BlockSpec(wqst.shape, const2),
        pl.BlockSpec(wk.shape, const3),
        pl.BlockSpec(wvt.shape, const2),
        pl.BlockSpec((tm, K_LANES), lambda b, i: (i, 0)),
        pl.BlockSpec((tm, K_LANES), lambda b, i: (i, 0)),
        pl.BlockSpec((D_QK, tm), lambda b, i: (0, i)),
        pl.BlockSpec((D_QK, tm), lambda b, i: (0, i)),
        pl.BlockSpec(cc.shape, const2),
        pl.BlockSpec(sc.shape, const2),
    ]
    tok_spec = lambda w: pl.BlockSpec((1, tm, w), lambda b, i: (b, i, 0))
    out_shape = [
        jax.ShapeDtypeStruct((bsz, N_HEADS, seq // tq, D_QK, tq), jnp.bfloat16),
        jax.ShapeDtypeStruct((bsz, N_HEADS, seq, K_LANES), jnp.bfloat16),
        jax.ShapeDtypeStruct((bsz, N_HEADS, seq // tk, V_ROWS, tk), jnp.bfloat16),
        jax.ShapeDtypeStruct((bsz, seq, D_CONV), jnp.float32),
        jax.ShapeDtypeStruct((bsz, seq, D_CONV), jnp.float32),
        jax.ShapeDtypeStruct((bsz, seq, 2 * D_FOURIER), jnp.bfloat16),
    ]
    out_specs = [pl.BlockSpec((1, N_HEADS, 1, D_QK, tm), lambda b, i: (b, 0, i // (tq // tm), 0, i % (tq // tm))),
                 pl.BlockSpec((1, N_HEADS, tm, K_LANES), lambda b, i: (b, 0, i, 0)),
                 pl.BlockSpec((1, N_HEADS, 1, V_ROWS, tm), lambda b, i: (b, 0, i // (tk // tm), 0, i % (tk // tm))),
                 tok_spec(D_CONV), tok_spec(D_CONV), tok_spec(2 * D_FOURIER)]
    if ln_input:
        out_shape = [jax.ShapeDtypeStruct((bsz, seq, d), jnp.float32)] + out_shape
        out_specs = [tok_spec(d)] + out_specs
    return pl.pallas_call(
        functools.partial(_inproj_kernel, ln_input=ln_input, q_scale=q_scale),
        grid=grid, in_specs=in_specs, out_specs=out_specs, out_shape=out_shape,
        compiler_params=_cparams(("parallel", "parallel")), name="inproj",
    )(x, lng, lnb, win, gq, gkv, wqt, wqst, wk, wvt, c96, s96, c96t, s96t, cc, sc)


def _attn_kernel(qt_ref, k_ref, vt_ref, o_ref, s0_ref, s1_ref, c0_ref, c1_ref, m_ref, acc_ref, *, tk, n_kv, n_q):
    n_t = n_q * n_kv

    rows = tk // ATTN_QK_PARTS

    def scores_part(t, part, s_ref, c_ref):
        off = pl.multiple_of((t % n_kv) * tk + part * rows, rows)
        s = _bdot(k_ref[0, 0, pl.ds(off, rows), :D_QK], qt_ref[0, 0, t // n_kv])
        s_ref[part * rows:(part + 1) * rows, :] = s
        cmax = jnp.max(s, axis=0, keepdims=True)
        c_ref[...] = cmax if part == 0 else jnp.maximum(c_ref[...], cmax)

    def stage_pair(t_next, s_next, c_next, t, s_ref, c_ref, with_scores=True):
        j = t % n_kv
        m_old = jnp.where(j == 0, -jnp.inf, m_ref[...])
        m_new = jnp.maximum(m_old, c_ref[...])
        acc = jnp.exp2(m_old - m_new) * acc_ref[...]
        for part in range(ATTN_QK_PARTS):
            if with_scores:
                scores_part(t_next, part, s_next, c_next)
            for lo in range(part * rows, (part + 1) * rows, MXU_DEPTH):
                p = jnp.exp2(s_ref[lo:lo + MXU_DEPTH, :] - m_new).astype(jnp.bfloat16)
                acc = acc + _bdot(vt_ref[0, 0, j, :, lo:lo + MXU_DEPTH], p)
        acc_ref[...] = acc
        m_ref[...] = m_new

    def emit(t):
        acc = acc_ref[...]
        o_ref[0, 0, t // n_kv] = (acc[:V_DIM] / acc[V_DIM:V_DIM + 1]).astype(o_ref.dtype)

    m_ref[...] = jnp.full(m_ref.shape, -jnp.inf, jnp.float32)
    acc_ref[...] = jnp.zeros(acc_ref.shape, jnp.float32)
    for part in range(ATTN_QK_PARTS):
        scores_part(0, part, s0_ref, c0_ref)

    def pair(i, carry):
        t = 2 * i
        stage_pair(t + 1, s1_ref, c1_ref, t, s0_ref, c0_ref)
        stage_pair(t + 2, s0_ref, c0_ref, t + 1, s1_ref, c1_ref)

        @pl.when((t + 1) % n_kv == n_kv - 1)
        def _():
            emit(t + 1)

        return carry

    lax.fori_loop(0, n_t // 2 - 1, pair, 0)
    stage_pair(n_t - 1, s1_ref, c1_ref, n_t - 2, s0_ref, c0_ref)
    stage_pair(0, s0_ref, c0_ref, n_t - 1, s1_ref, c1_ref, with_scores=False)
    emit(n_t - 1)


def _attention(qt, k, vt):
    bsz, nh, seq, _ = k.shape
    n_q, _, tq = qt.shape[2:]
    n_kv, _, tk = vt.shape[2:]
    assert n_kv % 2 == 0 and n_kv * tk == seq and n_q * tq == seq and tk % (ATTN_QK_PARTS * MXU_DEPTH) == 0
    return pl.pallas_call(
        functools.partial(_attn_kernel, tk=tk, n_kv=n_kv, n_q=n_q),
        grid=(bsz, nh),
        in_specs=[
            pl.BlockSpec((1, 1, n_q, D_QK, tq), lambda b, h: (b, h, 0, 0, 0)),
            pl.BlockSpec((1, 1, seq, K_LANES), lambda b, h: (b, h, 0, 0)),
            pl.BlockSpec((1, 1, n_kv, V_ROWS, tk), lambda b, h: (b, h, 0, 0, 0)),
        ],
        out_specs=pl.BlockSpec((1, 1, n_q, V_DIM, tq), lambda b, h: (b, h, 0, 0, 0)),
        out_shape=jax.ShapeDtypeStruct((bsz, nh, n_q, V_DIM, tq), jnp.bfloat16),
        scratch_shapes=[pltpu.VMEM((tk, tq), jnp.float32), pltpu.VMEM((tk, tq), jnp.float32),
                        pltpu.VMEM((1, tq), jnp.float32), pltpu.VMEM((1, tq), jnp.float32),
                        pltpu.VMEM((1, tq), jnp.float32), pltpu.VMEM((V_ROWS, tq), jnp.float32)],
        compiler_params=_cparams(("parallel", "parallel")), name="attention",
    )(qt, k, vt)


def _attn_kv_chunk(seq, tm, tq):
    tk = tm
    while seq % (4 * tk) == 0 and 2 * tk * tq * 4 <= ATTN_SCORE_BUF_BYTES:
        tk *= 2
    return tk


def _fourier_a_kernel(ab_ref, m1a_ref, m1b_ref, y_ref, *, n2c):
    for j in range(n2c):
        a = ab_ref[0, :, j * 2 * D_FOURIER: j * 2 * D_FOURIER + D_FOURIER]
        b = ab_ref[0, :, j * 2 * D_FOURIER + D_FOURIER: (j + 1) * 2 * D_FOURIER]
        y = _bdot(m1a_ref[...], a) + _bdot(m1b_ref[...], b)
        y_ref[0, :, j * D_FOURIER:(j + 1) * D_FOURIER] = y.astype(y_ref.dtype)


def _fourier_b_kernel(yr_ref, yi_ref, m2_ref, o_ref, *, k1c, n2):
    for j in range(k1c):
        m2 = m2_ref[j]
        z = _bdot(m2[:, :n2], yr_ref[0, j]) + _bdot(m2[:, n2:], yi_ref[0, j])
        o_ref[0, :, j, :] = z.astype(o_ref.dtype)


def _fourier(ab, m1a, m1b, m2):
    bsz, seq, _ = ab.shape
    n1 = DFT_N1
    n2 = seq // n1
    n2c = min(n2, 16)
    k1c = 8
    abv = ab.reshape(bsz, n1, n2 * 2 * D_FOURIER)
    y = pl.pallas_call(
        functools.partial(_fourier_a_kernel, n2c=n2c),
        grid=(bsz, n2 // n2c),
        in_specs=[
            pl.BlockSpec((1, n1, n2c * 2 * D_FOURIER), lambda b, c: (b, 0, c)),
            pl.BlockSpec(m1a.shape, lambda b, c: (0, 0)),
            pl.BlockSpec(m1b.shape, lambda b, c: (0, 0)),
        ],
        out_specs=pl.BlockSpec((1, 2 * n1, n2c * D_FOURIER), lambda b, c: (b, 0, c)),
        out_shape=jax.ShapeDtypeStruct((bsz, 2 * n1, n2 * D_FOURIER), jnp.bfloat16),
        compiler_params=_cparams(("parallel", "parallel")), name="fourier_a",
    )(abv, m1a, m1b)
    yv = y.reshape(bsz, 2 * n1, n2, D_FOURIER)
    nblk = n1 // k1c
    out = pl.pallas_call(
        functools.partial(_fourier_b_kernel, k1c=k1c, n2=n2),
        grid=(bsz, nblk),
        in_specs=[
            pl.BlockSpec((1, k1c, n2, D_FOURIER), lambda b, c: (b, c, 0, 0)),
            pl.BlockSpec((1, k1c, n2, D_FOURIER), lambda b, c: (b, nblk + c, 0, 0)),
            pl.BlockSpec((k1c, n2, 2 * n2), lambda b, c: (c, 0, 0)),
        ],
        out_specs=pl.BlockSpec((1, n2, k1c, D_FOURIER), lambda b, c: (b, 0, c, 0)),
        out_shape=jax.ShapeDtypeStruct((bsz, n2, n1, D_FOURIER), jnp.float32),
        compiler_params=_cparams(("parallel", "parallel")), name="fourier_b",
    )(yv, yv, m2)
    return out.reshape(bsz, seq, D_FOURIER)


def _dft_tables(seq):
    n1 = DFT_N1
    n2 = seq // n1
    c = np.arange(FOURIER_GROUP)
    ang = 2.0 * np.pi * np.outer(c, c) / FOURIER_GROUP
    eye = np.eye(D_FOURIER // FOURIER_GROUP)
    cc = np.kron(eye, np.cos(ang))
    sc = np.kron(eye, np.sin(ang))
    i1 = np.arange(n1)
    a1 = 2.0 * np.pi * np.outer(i1, i1) / n1
    c1, s1 = np.cos(a1), np.sin(a1)
    m1a = np.concatenate([c1, -s1], axis=0)
    m1b = np.concatenate([-s1, -c1], axis=0)
    k = i1[:, None] + n1 * np.arange(n2)[None, :]
    phi = 2.0 * np.pi * (k[:, :, None] * np.arange(n2)[None, None, :] % seq) / seq
    norm = 1.0 / math.sqrt(seq * FOURIER_GROUP)
    m2 = np.concatenate([np.cos(phi), np.sin(phi)], axis=-1) * norm
    bf = lambda t: jnp.asarray(t, jnp.float32).astype(jnp.bfloat16)
    return bf(cc), bf(sc), bf(m1a), bf(m1b), bf(m2)


def _outproj_kernel(o_ref, u_ref, up_ref, un_ref, b_ref, f_ref, h_ref, wom_ref, woc_ref, wof_ref, wconv_ref,
                    g_ref, beta_ref, wrh_ref, wrl_ref, br_ref, h1_ref, comb_ref, cnt_ref, *, alpha):
    i = pl.program_id(1)
    last = pl.num_programs(1) - 1
    tm = u_ref.shape[1]
    u = u_ref[0]
    up = up_ref[0][SUBLANES - 1:, :] * (i > 0).astype(jnp.float32)
    un = un_ref[0][0:1, :] * (i < last).astype(jnp.float32)
    row = lax.broadcasted_iota(jnp.int32, u.shape, 0)
    u_m1 = jnp.where(row == 0, up, pltpu.roll(u, 1, axis=0))
    u_p1 = jnp.where(row == tm - 1, un, pltpu.roll(u, tm - 1, axis=0))
    wc = wconv_ref[...]
    oconv = b_ref[0] * (u_m1 * wc[0:1, :] + u * wc[1:2, :] + u_p1 * wc[2:3, :])
    mix = _bdot(oconv.astype(jnp.bfloat16), woc_ref[...]) + _bdot(f_ref[0].astype(jnp.bfloat16), wof_ref[...])
    o_t = o_ref[0, :, 0].reshape(N_HEADS * V_DIM, o_ref.shape[-1])
    mix = mix + lax.dot_general(o_t, wom_ref[...], (((0,), (0,)), ((), ())),
                                preferred_element_type=jnp.float32)
    h1 = _layer_norm(alpha * h_ref[0] + mix, g_ref[...], beta_ref[...])
    h1_ref[0] = h1

    h_hi = h1.astype(jnp.bfloat16)
    h_lo = (h1 - h_hi.astype(jnp.float32)).astype(jnp.bfloat16)
    logits = (_bdot(h_hi, wrh_ref[...]) + (_bdot(h_hi, wrl_ref[...]) + _bdot(h_lo, wrh_ref[...]))) + br_ref[...]
    lane = lax.broadcasted_iota(jnp.int32, logits.shape, 1)
    neg = jnp.float32(-jnp.inf)
    big = jnp.int32(1 << 20)
    is_g = lane < N_GROUPS
    gl = jnp.where(is_g, logits, neg)
    gmax = jnp.max(gl, axis=-1, keepdims=True)
    p_group = 1.0 / jnp.sum(jnp.exp(gl - gmax), axis=-1, keepdims=True)
    g_sel = jnp.min(jnp.where(is_g & (logits == gmax), lane, big), axis=-1, keepdims=True)
    e_lo = N_GROUPS + g_sel * EXPERTS_PER_GROUP
    is_e = (lane >= e_lo) & (lane < e_lo + EXPERTS_PER_GROUP)
    l1 = jnp.max(jnp.where(is_e, logits, neg), axis=-1, keepdims=True)
    i1 = jnp.min(jnp.where(is_e & (logits == l1), lane, big), axis=-1, keepdims=True)
    is_e2 = is_e & (lane != i1)
    l2 = jnp.max(jnp.where(is_e2, logits, neg), axis=-1, keepdims=True)
    i2 = jnp.min(jnp.where(is_e2 & (logits == l2), lane, big), axis=-1, keepdims=True)
    r = jnp.exp(l2 - l1)
    w1 = 1.0 / (1.0 + r)
    w2 = r * w1
    comb = jnp.where(lane == i1, w1, jnp.where(lane == i2, w2, 0.0)) * p_group
    comb_ref[0] = jnp.where(lane == GID_LANE, g_sel.astype(jnp.float32), comb)
    cnt_ref[0, 0] = jnp.sum(jnp.where(lane == g_sel, 1.0, 0.0), axis=0, keepdims=True)


def _outproj(o, u, bgate, ofour, h, wom, woc, wof, wconv, g, beta, wrh, wrl, br, *, alpha, tm):
    bsz, seq, d = h.shape
    tq = o.shape[-1]
    n_row_blocks = seq // SUBLANES
    rows_per_tile = tm // SUBLANES
    const2 = lambda b, i: (0, 0)
    const3 = lambda b, i: (0, 0, 0)
    tok = lambda w: pl.BlockSpec((1, tm, w), lambda b, i: (b, i, 0))
    return pl.pallas_call(
        functools.partial(_outproj_kernel, alpha=alpha),
        grid=(bsz, seq // tm),
        in_specs=[
            pl.BlockSpec((1, N_HEADS, 1, V_DIM, tm), lambda b, i: (b, 0, i // (tq // tm), 0, i % (tq // tm))),
            tok(D_CONV),
            pl.BlockSpec((1, SUBLANES, D_CONV), lambda b, i: (b, jnp.maximum(i * rows_per_tile - 1, 0), 0)),
            pl.BlockSpec((1, SUBLANES, D_CONV),
                         lambda b, i: (b, jnp.minimum((i + 1) * rows_per_tile, n_row_blocks - 1), 0)),
            tok(D_CONV),
            tok(D_FOURIER),
            tok(d),
            pl.BlockSpec(wom.shape, const2),
            pl.BlockSpec(woc.shape, const2),
            pl.BlockSpec(wof.shape, const2),
            pl.BlockSpec(wconv.shape, const2),
            pl.BlockSpec((1, d), const2),
            pl.BlockSpec((1, d), const2),
            pl.BlockSpec(wrh.shape, const2),
            pl.BlockSpec(wrl.shape, const2),
            pl.BlockSpec((1, ROUTER_LANES), const2),
        ],
        out_specs=[tok(d), tok(ROUTER_LANES), pl.BlockSpec((1, 1, 1, ROUTER_LANES), lambda b, i: (b, i, 0, 0))],
        out_shape=[jax.ShapeDtypeStruct((bsz, seq, d), jnp.float32),
                   jax.ShapeDtypeStruct((bsz, seq, ROUTER_LANES), jnp.float32),
                   jax.ShapeDtypeStruct((bsz, seq // tm, 1, ROUTER_LANES), jnp.float32)],
        compiler_params=_cparams(("parallel", "parallel")), name="outproj",
    )(o, u, u, u, bgate, ofour, h, wom, woc, wof, wconv, g, beta, wrh, wrl, br)


def _dot_tn(a, b):
    return lax.dot_general(a, b, (((0,), (0,)), ((), ())), preferred_element_type=jnp.float32)


def _moe_kernel(offs_ref, h_ref, comb_ref, wgu_ref, wd_ref, g_ref, beta_ref, o_ref, pt_ref, xs_ref, cw_ref, ys_ref,
                *, alpha):
    i = pl.program_id(0)
    s = pl.program_id(1)
    n_blk = MOE_TILE // MOE_BLK
    e_step = EXPERTS_PER_GROUP
    f32, bf16 = jnp.float32, jnp.bfloat16

    @pl.when(s == 0)
    def _():
        comb = comb_ref[...]
        lane = lax.broadcasted_iota(jnp.int32, comb.shape, 1)
        in_grp = lane.astype(f32) == comb[:, GID_LANE:GID_LANE + 1]
        row = lax.broadcasted_iota(jnp.int32, (MOE_TILE, MOE_TILE), 0)
        col = lax.broadcasted_iota(jnp.int32, (MOE_TILE, MOE_TILE), 1)
        lower = jnp.where(row >= col, 1.0, 0.0).astype(bf16)
        rank = _bdot(lower, jnp.where(in_grp, 1.0, 0.0).astype(bf16))
        lane1 = lax.broadcasted_iota(jnp.int32, (1, ROUTER_LANES), 1)
        start = jnp.zeros((1, ROUTER_LANES), f32)
        for gg in range(N_GROUPS):
            start = jnp.where(lane1 == gg, offs_ref[i, gg].astype(f32), start)
        pos = jnp.sum(jnp.where(in_grp, start + rank - 1.0, 0.0), axis=-1, keepdims=True)
        pt = jnp.where(col.astype(f32) == pos, 1.0, 0.0).astype(bf16)
        pt_ref[...] = pt
        c1 = comb.astype(bf16)
        r1 = comb - c1.astype(f32)
        c2 = r1.astype(bf16)
        c3 = (r1 - c2.astype(f32)).astype(bf16)
        d = h_ref.shape[1]
        moved = _dot_tn(jnp.concatenate([h_ref[...].astype(bf16), c1, c2, c3], axis=1), pt)
        xs = moved[:d]
        cw = (moved[d:d + ROUTER_LANES] + moved[d + ROUTER_LANES:d + 2 * ROUTER_LANES]
              + moved[d + 2 * ROUTER_LANES:])
        for bb in range(n_blk):
            xs_ref[bb] = xs[:, bb * MOE_BLK:(bb + 1) * MOE_BLK].astype(bf16)
            cw_ref[bb] = cw[:, bb * MOE_BLK:(bb + 1) * MOE_BLK]
        ys_ref[...] = jnp.zeros_like(ys_ref)

    lo = offs_ref[i, s]
    hi = offs_ref[i, s + 1]
    b_lo = lo // MOE_BLK
    b_hi = jnp.where(hi > lo, (hi + MOE_BLK - 1) // MOE_BLK, b_lo)
    cw_row = N_GROUPS + s * e_step

    def block(b, carry):
        au = _bdot(wgu_ref[0], xs_ref[b])
        hid = []
        for e in range(e_step):
            a = au[2 * e * D_EXPERT:(2 * e + 1) * D_EXPERT]
            up = au[(2 * e + 1) * D_EXPERT:(2 * e + 2) * D_EXPERT]
            w = cw_ref[b, pl.ds(cw_row + e, 1), :]
            hid.append((a * (1.0 / (1.0 + jnp.exp(-a))) * up * w).astype(bf16))
        ys_ref[b] += _bdot(wd_ref[0], jnp.concatenate(hid, axis=0))
        return carry

    lax.fori_loop(b_lo, b_hi, block, 0)

    @pl.when(s == pl.num_programs(1) - 1)
    def _():
        ys = jnp.concatenate([ys_ref[bb] for bb in range(n_blk)], axis=1)
        y_hi = ys.astype(bf16)
        y_lo = (ys - y_hi.astype(f32)).astype(bf16)
        pt = pt_ref[...]
        y = _dot_nt(pt, y_hi) + _dot_nt(pt, y_lo)
        o_ref[...] = _layer_norm(alpha * h_ref[...] + y, g_ref[...], beta_ref[...])


def _moe(h, comb, offs, wgu, wd, g, beta, *, alpha):
    n, d = h.shape
    const2 = lambda i, s, offs: (0, 0)
    grid_spec = pltpu.PrefetchScalarGridSpec(
        num_scalar_prefetch=1,
        grid=(n // MOE_TILE, N_GROUPS),
        in_specs=[
            pl.BlockSpec((MOE_TILE, d), lambda i, s, offs: (i, 0), pipeline_mode=pl.Buffered(1)),
            pl.BlockSpec((MOE_TILE, ROUTER_LANES), lambda i, s, offs: (i, 0), pipeline_mode=pl.Buffered(1)),
            pl.BlockSpec((1,) + wgu.shape[1:], lambda i, s, offs: (s, 0, 0)),
            pl.BlockSpec((1,) + wd.shape[1:], lambda i, s, offs: (s, 0, 0)),
            pl.BlockSpec((1, d), const2),
            pl.BlockSpec((1, d), const2),
        ],
        out_specs=pl.BlockSpec((MOE_TILE, d), lambda i, s, offs: (i, 0)),
        scratch_shapes=[
            pltpu.VMEM((MOE_TILE, MOE_TILE), jnp.bfloat16),
            pltpu.VMEM((MOE_TILE // MOE_BLK, d, MOE_BLK), jnp.bfloat16),
            pltpu.VMEM((MOE_TILE // MOE_BLK, ROUTER_LANES, MOE_BLK), jnp.float32),
            pltpu.VMEM((MOE_TILE // MOE_BLK, d, MOE_BLK), jnp.float32),
        ],
    )
    return pl.pallas_call(
        functools.partial(_moe_kernel, alpha=alpha),
        grid_spec=grid_spec,
        out_shape=jax.ShapeDtypeStruct((n, d), jnp.float32),
        compiler_params=_cparams(("parallel", "arbitrary")), name="moe",
    )(offs, h, comb, wgu, wd, g, beta)


def _moe_weights_kernel(wg_ref, wu_ref, wd_ref, wgu_ref, wdt_ref):
    wgu_ref[0, 0] = wg_ref[0, 0, 0].T.astype(wgu_ref.dtype)
    wgu_ref[0, 1] = wu_ref[0, 0, 0].T.astype(wgu_ref.dtype)
    wdt_ref[0] = wd_ref[0, 0, 0].T.astype(wdt_ref.dtype)


def _moe_weights(w_gate, w_up, w_down, layer):
    _, ng, ne, d, f = w_gate.shape
    w_spec = lambda a, b: pl.BlockSpec((1, 1, 1, a, b), lambda g, e: (layer, g, e, 0, 0))
    wgu, wdt = pl.pallas_call(
        _moe_weights_kernel,
        grid=(ng, ne),
        in_specs=[w_spec(d, f), w_spec(d, f), w_spec(f, d)],
        out_specs=[pl.BlockSpec((1, 2, f, d), lambda g, e: (g * ne + e, 0, 0, 0)),
                   pl.BlockSpec((1, d, f), lambda g, e: (g, 0, e))],
        out_shape=[jax.ShapeDtypeStruct((ng * ne, 2, f, d), jnp.bfloat16),
                   jax.ShapeDtypeStruct((ng, d, ne * f), jnp.bfloat16)],
        compiler_params=_cparams(("parallel", "parallel")), name="moe_weights",
    )(w_gate, w_up, w_down)
    return wgu.reshape(ng, ne * 2 * f, d), wdt


def _group_offsets(cnt, tiles_per_moe_tile):
    c = cnt.reshape(-1, tiles_per_moe_tile, ROUTER_LANES)[:, :, :N_GROUPS].sum(axis=1)
    ends = jnp.cumsum(c, axis=-1)
    offs = jnp.concatenate([jnp.zeros_like(ends[:, :1]), ends], axis=-1)
    return offs.astype(jnp.int32)


def _rope_tables(seq):
    inv_freq = 1.0 / (ROPE_THETA ** (jnp.arange(0, QK_ROPE, 2, dtype=jnp.float32) / QK_ROPE))
    ang = jnp.arange(seq, dtype=jnp.float32)[:, None] * inv_freq[None, :]
    cos, sin = jnp.cos(ang), jnp.sin(ang)
    c96 = jnp.concatenate([jnp.ones((seq, QK_NOPE), jnp.float32), cos, cos], axis=-1)
    s96 = jnp.concatenate([jnp.zeros((seq, QK_NOPE), jnp.float32), -sin, sin], axis=-1)
    return _pad_cols(c96, 0, K_LANES), _pad_cols(s96, 0, K_LANES), c96.T, s96.T


def _swap_halves(w):
    half = w.shape[-1] // 2
    return jnp.concatenate([w[..., half:], w[..., :half]], axis=-1)


def _pad_cols(w, left, total):
    return jnp.pad(w, [(0, 0)] * (w.ndim - 1) + [(left, total - left - w.shape[-1])])


def _pick_tile(n, want):
    t = min(n, want)
    while n % t:
        t //= 2
    return t


def kernel(x, ln_in_g, ln_in_b, w_in, g_q, g_kv, w_uq, w_ukv, w_conv, w_out, ln1_g, ln1_b, w_group, b_group,
           w_router, b_router, w_gate, w_up, w_down, ln2_g, ln2_b):
    bsz, seq, d = x.shape
    depth = w_in.shape[0]
    alpha = (2.0 * depth) ** 0.25
    bf16 = jnp.bfloat16
    tm = _pick_tile(seq, 512)
    tq = _pick_tile(seq, 1024)
    tk = _attn_kv_chunk(seq, tm, tq)

    c96, s96, c96t, s96t = _rope_tables(seq)
    cc, sc, m1a, m1b, m2 = _dft_tables(seq)
    row = lambda v: v.reshape(1, -1)
    d_mla = N_HEADS * V_DIM

    h = x
    for l in range(depth):
        wi = w_in[l]
        splits = np.cumsum([Q_LORA, KV_LORA, QK_ROPE, D_CONV, D_CONV, D_CONV])
        w_cq, w_ckv, w_kpe, w_b, w_c, w_h, w_f = jnp.split(wi, splits, axis=-1)
        win = jnp.concatenate([
            w_cq, w_ckv, w_b, w_c, w_h, w_f,
            _pad_cols(w_kpe, QK_NOPE, K_LANES), _pad_cols(_swap_halves(w_kpe), QK_NOPE, K_LANES)],
            axis=-1).astype(bf16)
        wq = jnp.transpose(w_uq[l], (1, 0, 2))
        wqs = _pad_cols(_swap_halves(wq[..., QK_NOPE:]), QK_NOPE, D_QK)
        wqt = jnp.transpose(wq, (0, 2, 1)).reshape(N_HEADS * D_QK, Q_LORA)
        wqst = jnp.transpose(wqs, (0, 2, 1)).reshape(N_HEADS * D_QK, Q_LORA)
        wkv = jnp.transpose(w_ukv[l], (1, 0, 2))
        wk = _pad_cols(wkv[..., :QK_NOPE], 0, K_LANES)
        wvt = _pad_cols(wkv[..., QK_NOPE:], 0, V_ROWS)
        wvt = jnp.transpose(wvt, (0, 2, 1)).reshape(N_HEADS * V_ROWS, KV_LORA)
        outs = _inproj(h, row(ln_in_g), row(ln_in_b), win, row(g_q[l]), row(g_kv[l]), wqt.astype(bf16),
                       wqst.astype(bf16), wk.astype(bf16), wvt.astype(bf16), c96, s96, c96t, s96t, cc, sc,
                       ln_input=(l == 0), tm=tm, tq=tq, tk=tk)
        if l == 0:
            h, outs = outs[0], outs[1:]
        qt, k, vt, u, bgate, ab = outs
        o_mla = _attention(qt, k, vt)
        o_four = _fourier(ab, m1a, m1b, m2)

        wo = w_out[l].astype(bf16)
        wom = wo[:d_mla]
        woc = wo[d_mla:d_mla + D_CONV]
        wof = wo[d_mla + D_CONV:]
        wr = _pad_cols(jnp.concatenate([w_group[l], w_router[l].reshape(d, N_EXPERTS)], axis=-1), 0, ROUTER_LANES)
        br = _pad_cols(jnp.concatenate([b_group[l], b_router[l].reshape(N_EXPERTS)])[None, :], 0, ROUTER_LANES)
        wrh = wr.astype(bf16)
        wrl = (wr - wrh.astype(jnp.float32)).astype(bf16)
        h1, comb, cnt = _outproj(o_mla, u, bgate, o_four, h, wom, woc, wof, w_conv[l], row(ln1_g[l]), row(ln1_b[l]),
                            wrh, wrl, br, alpha=alpha, tm=tm)

        wgu, wd = _moe_weights(w_gate, w_up, w_down, l)
        offs = _group_offsets(cnt, MOE_TILE // tm)
        h = _moe(h1.reshape(bsz * seq, d), comb.reshape(bsz * seq, ROUTER_LANES), offs, wgu, wd,
                 row(ln2_g[l]), row(ln2_b[l]), alpha=alpha).reshape(bsz, seq, d)
    return h
```

```python
import functools
import math

import numpy as np
import jax
import jax.numpy as jnp
from jax import lax
from jax.experimental import pallas as pl
from jax.experimental.pallas import tpu as pltpu

N_HEADS = 8
QK_NOPE = 64
QK_ROPE = 32
D_QK = QK_NOPE + QK_ROPE
K_LANES = 128
V_DIM = 64
V_ROWS = 80
Q_LORA = 384
KV_LORA = 256
D_CONV = 256
D_FOURIER = 256
FOURIER_GROUP = 64
N_GROUPS = 4
EXPERTS_PER_GROUP = 8
N_EXPERTS = N_GROUPS * EXPERTS_PER_GROUP
D_EXPERT = 256
ROPE_THETA = 10000.0
LN_EPS = 1e-5
RMS_EPS = 1e-6
DFT_N1 = 128
ROUTER_LANES = 128
GID_LANE = N_GROUPS + N_EXPERTS
MOE_TILE = 1024
MOE_BLK = 256
MOE_LOOKAHEAD = 5
SUBLANES = 8
V7X_VMEM_BYTES = 64 * 1024 * 1024
VMEM_LIMIT = V7X_VMEM_BYTES - 4 * 1024 * 1024
ATTN_SCORE_BUF_BYTES = 8 * 1024 * 1024
ATTN_QK_PARTS = 2
MXU_DEPTH = 256

_O_CQ = 0
_O_CKV = _O_CQ + Q_LORA
_O_B = _O_CKV + KV_LORA
_O_C = _O_B + D_CONV
_O_H = _O_C + D_CONV
_O_F = _O_H + D_CONV
_O_KPE = _O_F + D_FOURIER
_O_KPES = _O_KPE + K_LANES
_W_IN_COLS = _O_KPES + K_LANES


def _cparams(sem):
    return pltpu.CompilerParams(dimension_semantics=sem, vmem_limit_bytes=VMEM_LIMIT)


def _layer_norm(x, g, b):
    mu = jnp.mean(x, axis=-1, keepdims=True)
    xc = x - mu
    var = jnp.mean(xc * xc, axis=-1, keepdims=True)
    return xc * lax.rsqrt(var + LN_EPS) * g + b


def _rms_norm(x, g):
    return x * lax.rsqrt(jnp.mean(x * x, axis=-1, keepdims=True) + RMS_EPS) * g


def _bdot(a, b):
    return jnp.dot(a, b, preferred_element_type=jnp.float32)


def _dot_nt(a, b):
    return lax.dot_general(a, b, (((1,), (1,)), ((), ())), preferred_element_type=jnp.float32)


def _inproj_kernel(x_ref, lng_ref, lnb_ref, win_ref, gq_ref, gkv_ref, wqt_ref, wqst_ref, wk_ref, wvt_ref,
                   c96_ref, s96_ref, c96t_ref, s96t_ref, cc_ref, sc_ref, *out_refs, ln_input, q_scale):
    if ln_input:
        h_ref, qt_ref, k_ref, vt_ref, u_ref, b_ref, ab_ref = out_refs
        h = _layer_norm(x_ref[0], lng_ref[...], lnb_ref[...])
        h_ref[0] = h
    else:
        qt_ref, k_ref, vt_ref, u_ref, b_ref, ab_ref = out_refs
        h = x_ref[0]
    z = _bdot(h.astype(jnp.bfloat16), win_ref[...])
    c96 = c96_ref[...]
    s96 = s96_ref[...]
    c96t = c96t_ref[...]
    s96t = s96t_ref[...]
    cqn = _rms_norm(z[:, _O_CQ:_O_CQ + Q_LORA], gq_ref[...]).astype(jnp.bfloat16)
    ckvn = _rms_norm(z[:, _O_CKV:_O_CKV + KV_LORA], gkv_ref[...]).astype(jnp.bfloat16)
    kpe = z[:, _O_KPE:_O_KPE + K_LANES] * c96 + z[:, _O_KPES:_O_KPES + K_LANES] * s96
    qt_all = _dot_nt(wqt_ref[...], cqn)
    qst_all = _dot_nt(wqst_ref[...], cqn)
    vt_all = _dot_nt(wvt_ref[...], ckvn)
    vrow = lax.broadcasted_iota(jnp.int32, (V_ROWS, vt_all.shape[1]), 0)
    for hd in range(N_HEADS):
        qt = (qt_all[hd * D_QK:(hd + 1) * D_QK] * c96t + qst_all[hd * D_QK:(hd + 1) * D_QK] * s96t) * q_scale
        qt_ref[0, hd, 0] = qt.astype(qt_ref.dtype)
        k_ref[0, hd] = (_bdot(ckvn, wk_ref[hd]) + kpe).astype(k_ref.dtype)
        vt = jnp.where(vrow == V_DIM, 1.0, vt_all[hd * V_ROWS:(hd + 1) * V_ROWS])
        vt_ref[0, hd, 0] = vt.astype(vt_ref.dtype)
    u_ref[0] = z[:, _O_C:_O_C + D_CONV] * z[:, _O_H:_O_H + D_CONV]
    b_ref[0] = z[:, _O_B:_O_B + D_CONV]
    fb = z[:, _O_F:_O_F + D_FOURIER].astype(jnp.bfloat16)
    ab_ref[0, :, :D_FOURIER] = _bdot(fb, cc_ref[...]).astype(ab_ref.dtype)
    ab_ref[0, :, D_FOURIER:] = _bdot(fb, sc_ref[...]).astype(ab_ref.dtype)


def _inproj(x, lng, lnb, win, gq, gkv, wqt, wqst, wk, wvt, c96, s96, c96t, s96t, cc, sc, *, ln_input, tm, tq, tk):
    bsz, seq, d = x.shape
    q_scale = (D_QK ** -0.5) * math.log2(math.e)
    grid = (bsz, seq // tm)
    const2 = lambda b, i: (0, 0)
    const3 = lambda b, i: (0, 0, 0)
    in_specs = [
        pl.BlockSpec((1, tm, d), lambda b, i: (b, i, 0)),
        pl.BlockSpec((1, d), const2),
        pl.BlockSpec((1, d), const2),
        pl.BlockSpec(win.shape, const2),
        pl.BlockSpec((1, Q_LORA), const2),
        pl.BlockSpec((1, KV_LORA), const2),
        pl.BlockSpec(wqt.shape, const2),
        pl.BlockSpec(wqst.shape, const2),
        pl.BlockSpec(wk.shape, const3),
        pl.BlockSpec(wvt.shape, const2),
        pl.BlockSpec((tm, K_LANES), lambda b, i: (i, 0)),
        pl.BlockSpec((tm, K_LANES), lambda b, i: (i, 0)),
        pl.BlockSpec((D_QK, tm), lambda b, i: (0, i)),
        pl.BlockSpec((D_QK, tm), lambda b, i: (0, i)),
        pl.BlockSpec(cc.shape, const2),
        pl.BlockSpec(sc.shape, const2),
    ]
    tok_spec = lambda w: pl.BlockSpec((1, tm, w), lambda b, i: (b, i, 0))
    out_shape = [
        jax.ShapeDtypeStruct((bsz, N_HEADS, seq // tq, D_QK, tq), jnp.bfloat16),
        jax.ShapeDtypeStruct((bsz, N_HEADS, seq, K_LANES), jnp.bfloat16),
        jax.ShapeDtypeStruct((bsz, N_HEADS, seq // tk, V_ROWS, tk), jnp.bfloat16),
        jax.ShapeDtypeStruct((bsz, seq, D_CONV), jnp.float32),
        jax.ShapeDtypeStruct((bsz, seq, D_CONV), jnp.float32),
        jax.ShapeDtypeStruct((bsz, seq, 2 * D_FOURIER), jnp.bfloat16),
    ]
    out_specs = [pl.BlockSpec((1, N_HEADS, 1, D_QK, tm), lambda b, i: (b, 0, i // (tq // tm), 0, i % (tq // tm))),
                 pl.BlockSpec((1, N_HEADS, tm, K_LANES), lambda b, i: (b, 0, i, 0)),
                 pl.BlockSpec((1, N_HEADS, 1, V_ROWS, tm), lambda b, i: (b, 0, i // (tk // tm), 0, i % (tk // tm))),
                 tok_spec(D_CONV), tok_spec(D_CONV), tok_spec(2 * D_FOURIER)]
    if ln_input:
        out_shape = [jax.ShapeDtypeStruct((bsz, seq, d), jnp.float32)] + out_shape
        out_specs = [tok_spec(d)] + out_specs
    return pl.pallas_call(
        functools.partial(_inproj_kernel, ln_input=ln_input, q_scale=q_scale),
        grid=grid, in_specs=in_specs, out_specs=out_specs, out_shape=out_shape,
        compiler_params=_cparams(("parallel", "parallel")), name="inproj",
    )(x, lng, lnb, win, gq, gkv, wqt, wqst, wk, wvt, c96, s96, c96t, s96t, cc, sc)


def _attn_kernel(qt_ref, k_ref, vt_ref, o_ref, s0_ref, s1_ref, c0_ref, c1_ref, m_ref, acc_ref, *, tk, n_kv, n_q):
    n_t = n_q * n_kv

    rows = tk // ATTN_QK_PARTS

    def scores_part(t, part, s_ref, c_ref):
        off = pl.multiple_of((t % n_kv) * tk + part * rows, rows)
        s = _bdot(k_ref[0, 0, pl.ds(off, rows), :D_QK], qt_ref[0, 0, t // n_kv])
        s_ref[part * rows:(part + 1) * rows, :] = s
        cmax = jnp.max(s, axis=0, keepdims=True)
        c_ref[...] = cmax if part == 0 else jnp.maximum(c_ref[...], cmax)

    def stage_pair(t_next, s_next, c_next, t, s_ref, c_ref, with_scores=True):
        j = t % n_kv
        m_old = jnp.where(j == 0, -jnp.inf, m_ref[...])
        m_new = jnp.maximum(m_old, c_ref[...])
        acc = jnp.exp2(m_old - m_new) * acc_ref[...]
        for part in range(ATTN_QK_PARTS):
            if with_scores:
                scores_part(t_next, part, s_next, c_next)
            for lo in range(part * rows, (part + 1) * rows, MXU_DEPTH):
                p = jnp.exp2(s_ref[lo:lo + MXU_DEPTH, :] - m_new).astype(jnp.bfloat16)
                acc = acc + _bdot(vt_ref[0, 0, j, :, lo:lo + MXU_DEPTH], p)
        acc_ref[...] = acc
        m_ref[...] = m_new

    def emit(t):
        acc = acc_ref[...]
        o_ref[0, 0, t // n_kv] = (acc[:V_DIM] / acc[V_DIM:V_DIM + 1]).astype(o_ref.dtype)

    m_ref[...] = jnp.full(m_ref.shape, -jnp.inf, jnp.float32)
    acc_ref[...] = jnp.zeros(acc_ref.shape, jnp.float32)
    for part in range(ATTN_QK_PARTS):
        scores_part(0, part, s0_ref, c0_ref)

    def pair(i, carry):
        t = 2 * i
        stage_pair(t + 1, s1_ref, c1_ref, t, s0_ref, c0_ref)
        stage_pair(t + 2, s0_ref, c0_ref, t + 1, s1_ref, c1_ref)

        @pl.when((t + 1) % n_kv == n_kv - 1)
        def _():
            emit(t + 1)

        return carry

    lax.fori_loop(0, n_t // 2 - 1, pair, 0)
    stage_pair(n_t - 1, s1_ref, c1_ref, n_t - 2, s0_ref, c0_ref)
    stage_pair(0, s0_ref, c0_ref, n_t - 1, s1_ref, c1_ref, with_scores=False)
    emit(n_t - 1)


def _attention(qt, k, vt):
    bsz, nh, seq, _ = k.shape
    n_q, _, tq = qt.shape[2:]
    n_kv, _, tk = vt.shape[2:]
    assert n_kv % 2 == 0 and n_kv * tk == seq and n_q * tq == seq and tk % (ATTN_QK_PARTS * MXU_DEPTH) == 0
    return pl.pallas_call(
        functools.partial(_attn_kernel, tk=tk, n_kv=n_kv, n_q=n_q),
        grid=(bsz, nh),
        in_specs=[
            pl.BlockSpec((1, 1, n_q, D_QK, tq), lambda b, h: (b, h, 0, 0, 0)),
            pl.BlockSpec((1, 1, seq, K_LANES), lambda b, h: (b, h, 0, 0)),
            pl.BlockSpec((1, 1, n_kv, V_ROWS, tk), lambda b, h: (b, h, 0, 0, 0)),
        ],
        out_specs=pl.BlockSpec((1, 1, n_q, V_DIM, tq), lambda b, h: (b, h, 0, 0, 0)),
        out_shape=jax.ShapeDtypeStruct((bsz, nh, n_q, V_DIM, tq), jnp.bfloat16),
        scratch_shapes=[pltpu.VMEM((tk, tq), jnp.float32), pltpu.VMEM((tk, tq), jnp.float32),
                        pltpu.VMEM((1, tq), jnp.float32), pltpu.VMEM((1, tq), jnp.float32),
                        pltpu.VMEM((1, tq), jnp.float32), pltpu.VMEM((V_ROWS, tq), jnp.float32)],
        compiler_params=_cparams(("parallel", "parallel")), name="attention",
    )(qt, k, vt)


def _attn_kv_chunk(seq, tm, tq):
    tk = tm
    while seq % (4 * tk) == 0 and 2 * tk * tq * 4 <= ATTN_SCORE_BUF_BYTES:
        tk *= 2
    return tk


def _fourier_a_kernel(ab_ref, m1a_ref, m1b_ref, y_ref, *, n2c):
    for j in range(n2c):
        a = ab_ref[0, :, j * 2 * D_FOURIER: j * 2 * D_FOURIER + D_FOURIER]
        b = ab_ref[0, :, j * 2 * D_FOURIER + D_FOURIER: (j + 1) * 2 * D_FOURIER]
        y = _bdot(m1a_ref[...], a) + _bdot(m1b_ref[...], b)
        y_ref[0, :, j * D_FOURIER:(j + 1) * D_FOURIER] = y.astype(y_ref.dtype)


def _fourier_b_kernel(yr_ref, yi_ref, m2_ref, o_ref, *, k1c, n2):
    for j in range(k1c):
        m2 = m2_ref[j]
        z = _bdot(m2[:, :n2], yr_ref[0, j]) + _bdot(m2[:, n2:], yi_ref[0, j])
        o_ref[0, :, j, :] = z.astype(o_ref.dtype)


def _fourier(ab, m1a, m1b, m2):
    bsz, seq, _ = ab.shape
    n1 = DFT_N1
    n2 = seq // n1
    n2c = min(n2, 16)
    k1c = 8
    abv = ab.reshape(bsz, n1, n2 * 2 * D_FOURIER)
    y = pl.pallas_call(
        functools.partial(_fourier_a_kernel, n2c=n2c),
        grid=(bsz, n2 // n2c),
        in_specs=[
            pl.BlockSpec((1, n1, n2c * 2 * D_FOURIER), lambda b, c: (b, 0, c)),
            pl.BlockSpec(m1a.shape, lambda b, c: (0, 0)),
            pl.BlockSpec(m1b.shape, lambda b, c: (0, 0)),
        ],
        out_specs=pl.BlockSpec((1, 2 * n1, n2c * D_FOURIER), lambda b, c: (b, 0, c)),
        out_shape=jax.ShapeDtypeStruct((bsz, 2 * n1, n2 * D_FOURIER), jnp.bfloat16),
        compiler_params=_cparams(("parallel", "parallel")), name="fourier_a",
    )(abv, m1a, m1b)
    yv = y.reshape(bsz, 2 * n1, n2, D_FOURIER)
    nblk = n1 // k1c
    out = pl.pallas_call(
        functools.partial(_fourier_b_kernel, k1c=k1c, n2=n2),
        grid=(bsz, nblk),
        in_specs=[
            pl.BlockSpec((1, k1c, n2, D_FOURIER), lambda b, c: (b, c, 0, 0)),
            pl.BlockSpec((1, k1c, n2, D_FOURIER), lambda b, c: (b, nblk + c, 0, 0)),
            pl.BlockSpec((k1c, n2, 2 * n2), lambda b, c: (c, 0, 0)),
        ],
        out_specs=pl.BlockSpec((1, n2, k1c, D_FOURIER), lambda b, c: (b, 0, c, 0)),
        out_shape=jax.ShapeDtypeStruct((bsz, n2, n1, D_FOURIER), jnp.float32),
        compiler_params=_cparams(("parallel", "parallel")), name="fourier_b",
    )(yv, yv, m2)
    return out.reshape(bsz, seq, D_FOURIER)


def _dft_tables(seq):
    n1 = DFT_N1
    n2 = seq // n1
    c = np.arange(FOURIER_GROUP)
    ang = 2.0 * np.pi * np.outer(c, c) / FOURIER_GROUP
    eye = np.eye(D_FOURIER // FOURIER_GROUP)
    cc = np.kron(eye, np.cos(ang))
    sc = np.kron(eye, np.sin(ang))
    i1 = np.arange(n1)
    a1 = 2.0 * np.pi * np.outer(i1, i1) / n1
    c1, s1 = np.cos(a1), np.sin(a1)
    m1a = np.concatenate([c1, -s1], axis=0)
    m1b = np.concatenate([-s1, -c1], axis=0)
    k = i1[:, None] + n1 * np.arange(n2)[None, :]
    phi = 2.0 * np.pi * (k[:, :, None] * np.arange(n2)[None, None, :] % seq) / seq
    norm = 1.0 / math.sqrt(seq * FOURIER_GROUP)
    m2 = np.concatenate([np.cos(phi), np.sin(phi)], axis=-1) * norm
    bf = lambda t: jnp.asarray(t, jnp.float32).astype(jnp.bfloat16)
    return bf(cc), bf(sc), bf(m1a), bf(m1b), bf(m2)


def _outproj_kernel(o_ref, u_ref, up_ref, un_ref, b_ref, f_ref, h_ref, wom_ref, woc_ref, wof_ref, wconv_ref,
                    g_ref, beta_ref, wrh_ref, wrl_ref, br_ref, h1_ref, comb_ref, cnt_ref, *, alpha):
    i = pl.program_id(1)
    last = pl.num_programs(1) - 1
    tm = u_ref.shape[1]
    u = u_ref[0]
    up = up_ref[0][SUBLANES - 1:, :] * (i > 0).astype(jnp.float32)
    un = un_ref[0][0:1, :] * (i < last).astype(jnp.float32)
    row = lax.broadcasted_iota(jnp.int32, u.shape, 0)
    u_m1 = jnp.where(row == 0, up, pltpu.roll(u, 1, axis=0))
    u_p1 = jnp.where(row == tm - 1, un, pltpu.roll(u, tm - 1, axis=0))
    wc = wconv_ref[...]
    oconv = b_ref[0] * (u_m1 * wc[0:1, :] + u * wc[1:2, :] + u_p1 * wc[2:3, :])
    mix = _bdot(oconv.astype(jnp.bfloat16), woc_ref[...]) + _bdot(f_ref[0].astype(jnp.bfloat16), wof_ref[...])
    o_t = o_ref[0, :, 0].reshape(N_HEADS * V_DIM, o_ref.shape[-1])
    mix = mix + lax.dot_general(o_t, wom_ref[...], (((0,), (0,)), ((), ())),
                                preferred_element_type=jnp.float32)
    h1 = _layer_norm(alpha * h_ref[0] + mix, g_ref[...], beta_ref[...])
    h1_ref[0] = h1

    h_hi = h1.astype(jnp.bfloat16)
    h_lo = (h1 - h_hi.astype(jnp.float32)).astype(jnp.bfloat16)
    logits = (_bdot(h_hi, wrh_ref[...]) + (_bdot(h_hi, wrl_ref[...]) + _bdot(h_lo, wrh_ref[...]))) + br_ref[...]
    lane = lax.broadcasted_iota(jnp.int32, logits.shape, 1)
    neg = jnp.float32(-jnp.inf)
    big = jnp.int32(1 << 20)
    is_g = lane < N_GROUPS
    gl = jnp.where(is_g, logits, neg)
    gmax = jnp.max(gl, axis=-1, keepdims=True)
    p_group = 1.0 / jnp.sum(jnp.exp(gl - gmax), axis=-1, keepdims=True)
    g_sel = jnp.min(jnp.where(is_g & (logits == gmax), lane, big), axis=-1, keepdims=True)
    e_lo = N_GROUPS + g_sel * EXPERTS_PER_GROUP
    is_e = (lane >= e_lo) & (lane < e_lo + EXPERTS_PER_GROUP)
    l1 = jnp.max(jnp.where(is_e, logits, neg), axis=-1, keepdims=True)
    i1 = jnp.min(jnp.where(is_e & (logits == l1), lane, big), axis=-1, keepdims=True)
    is_e2 = is_e & (lane != i1)
    l2 = jnp.max(jnp.where(is_e2, logits, neg), axis=-1, keepdims=True)
    i2 = jnp.min(jnp.where(is_e2 & (logits == l2), lane, big), axis=-1, keepdims=True)
    r = jnp.exp(l2 - l1)
    w1 = 1.0 / (1.0 + r)
    w2 = r * w1
    comb = jnp.where(lane == i1, w1, jnp.where(lane == i2, w2, 0.0)) * p_group
    comb_ref[0] = jnp.where(lane == GID_LANE, g_sel.astype(jnp.float32), comb)
    cnt_ref[0, 0] = jnp.sum(jnp.where(lane == g_sel, 1.0, 0.0), axis=0, keepdims=True)


def _outproj(o, u, bgate, ofour, h, wom, woc, wof, wconv, g, beta, wrh, wrl, br, *, alpha, tm):
    bsz, seq, d = h.shape
    tq = o.shape[-1]
    n_row_blocks = seq // SUBLANES
    rows_per_tile = tm // SUBLANES
    const2 = lambda b, i: (0, 0)
    const3 = lambda b, i: (0, 0, 0)
    tok = lambda w: pl.BlockSpec((1, tm, w), lambda b, i: (b, i, 0))
    return pl.pallas_call(
        functools.partial(_outproj_kernel, alpha=alpha),
        grid=(bsz, seq // tm),
        in_specs=[
            pl.BlockSpec((1, N_HEADS, 1, V_DIM, tm), lambda b, i: (b, 0, i // (tq // tm), 0, i % (tq // tm))),
            tok(D_CONV),
            pl.BlockSpec((1, SUBLANES, D_CONV), lambda b, i: (b, jnp.maximum(i * rows_per_tile - 1, 0), 0)),
            pl.BlockSpec((1, SUBLANES, D_CONV),
                         lambda b, i: (b, jnp.minimum((i + 1) * rows_per_tile, n_row_blocks - 1), 0)),
            tok(D_CONV),
            tok(D_FOURIER),
            tok(d),
            pl.BlockSpec(wom.shape, const2),
            pl.BlockSpec(woc.shape, const2),
            pl.BlockSpec(wof.shape, const2),
            pl.BlockSpec(wconv.shape, const2),
            pl.BlockSpec((1, d), const2),
            pl.BlockSpec((1, d), const2),
            pl.BlockSpec(wrh.shape, const2),
            pl.BlockSpec(wrl.shape, const2),
            pl.BlockSpec((1, ROUTER_LANES), const2),
        ],
        out_specs=[tok(d), tok(ROUTER_LANES), pl.BlockSpec((1, 1, 1, ROUTER_LANES), lambda b, i: (b, i, 0, 0))],
        out_shape=[jax.ShapeDtypeStruct((bsz, seq, d), jnp.float32),
                   jax.ShapeDtypeStruct((bsz, seq, ROUTER_LANES), jnp.float32),
                   jax.ShapeDtypeStruct((bsz, seq // tm, 1, ROUTER_LANES), jnp.float32)],
        compiler_params=_cparams(("parallel", "parallel")), name="outproj",
    )(o, u, u, u, bgate, ofour, h, wom, woc, wof, wconv, g, beta, wrh, wrl, br)


def _dot_tn(a, b):
    return lax.dot_general(a, b, (((0,), (0,)), ((), ())), preferred_element_type=jnp.float32)


def _moe_kernel(offs_ref, h_ref, comb_ref, wgu_ref, wd_ref, g_ref, beta_ref, o_ref, pt_ref, xs_ref, cw_ref, ys_ref,
                *, alpha):
    i = pl.program_id(0)
    s = pl.program_id(1)
    n_blk = MOE_TILE // MOE_BLK
    e_step = EXPERTS_PER_GROUP
    f32, bf16 = jnp.float32, jnp.bfloat16

    @pl.when(s == 0)
    def _():
        comb = comb_ref[...]
        lane = lax.broadcasted_iota(jnp.int32, comb.shape, 1)
        in_grp = lane.astype(f32) == comb[:, GID_LANE:GID_LANE + 1]
        row = lax.broadcasted_iota(jnp.int32, (MOE_TILE, MOE_TILE), 0)
        col = lax.broadcasted_iota(jnp.int32, (MOE_TILE, MOE_TILE), 1)
        lower = jnp.where(row >= col, 1.0, 0.0).astype(bf16)
        rank = _bdot(lower, jnp.where(in_grp, 1.0, 0.0).astype(bf16))
        lane1 = lax.broadcasted_iota(jnp.int32, (1, ROUTER_LANES), 1)
        start = jnp.zeros((1, ROUTER_LANES), f32)
        for gg in range(N_GROUPS):
            start = jnp.where(lane1 == gg, offs_ref[i, gg].astype(f32), start)
        pos = jnp.sum(jnp.where(in_grp, start + rank - 1.0, 0.0), axis=-1, keepdims=True)
        pt = jnp.where(col.astype(f32) == pos, 1.0, 0.0).astype(bf16)
        pt_ref[...] = pt
        c1 = comb.astype(bf16)
        r1 = comb - c1.astype(f32)
        c2 = r1.astype(bf16)
        c3 = (r1 - c2.astype(f32)).astype(bf16)
        d = h_ref.shape[1]
        moved = _dot_tn(jnp.concatenate([h_ref[...].astype(bf16), c1, c2, c3], axis=1), pt)
        xs = moved[:d]
        cw = (moved[d:d + ROUTER_LANES] + moved[d + ROUTER_LANES:d + 2 * ROUTER_LANES]
              + moved[d + 2 * ROUTER_LANES:])
        for bb in range(n_blk):
            xs_ref[bb] = xs[:, bb * MOE_BLK:(bb + 1) * MOE_BLK].astype(bf16)
            cw_ref[bb] = cw[:, bb * MOE_BLK:(bb + 1) * MOE_BLK]
        ys_ref[...] = jnp.zeros_like(ys_ref)

    lo = offs_ref[i, s]
    hi = offs_ref[i, s + 1]
    b_lo = lo // MOE_BLK
    b_hi = jnp.where(hi > lo, (hi + MOE_BLK - 1) // MOE_BLK, b_lo)
    cw_row = N_GROUPS + s * e_step

    def block(b, carry):
        xs = xs_ref[b]

        def gate_up(e):
            return _bdot(wgu_ref[0, 2 * e * D_EXPERT:(2 * e + 2) * D_EXPERT, :], xs)

        def down(e, au):
            a = au[:D_EXPERT]
            up = au[D_EXPERT:]
            w = cw_ref[b, pl.ds(cw_row + e, 1), :]
            hid = (a * (1.0 / (1.0 + jnp.exp(-a))) * up * w).astype(bf16)
            return _bdot(wd_ref[0, :, e * D_EXPERT:(e + 1) * D_EXPERT], hid)

        y = ys_ref[b]
        aus = [gate_up(e) for e in range(MOE_LOOKAHEAD)]
        for e in range(e_step):
            if e + MOE_LOOKAHEAD < e_step:
                aus.append(gate_up(e + MOE_LOOKAHEAD))
            y = y + down(e, aus[e])
        ys_ref[b] = y
        return carry

    lax.fori_loop(b_lo, b_hi, block, 0)

    @pl.when(s == pl.num_programs(1) - 1)
    def _():
        ys = jnp.concatenate([ys_ref[bb] for bb in range(n_blk)], axis=1)
        y_hi = ys.astype(bf16)
        y_lo = (ys - y_hi.astype(f32)).astype(bf16)
        pt = pt_ref[...]
        y = _dot_nt(pt, y_hi) + _dot_nt(pt, y_lo)
        o_ref[...] = _layer_norm(alpha * h_ref[...] + y, g_ref[...], beta_ref[...])


def _moe(h, comb, offs, wgu, wd, g, beta, *, alpha):
    n, d = h.shape
    const2 = lambda i, s, offs: (0, 0)
    grid_spec = pltpu.PrefetchScalarGridSpec(
        num_scalar_prefetch=1,
        grid=(n // MOE_TILE, N_GROUPS),
        in_specs=[
            pl.BlockSpec((MOE_TILE, d), lambda i, s, offs: (i, 0)),
            pl.BlockSpec((MOE_TILE, ROUTER_LANES), lambda i, s, offs: (i, 0)),
            pl.BlockSpec((1,) + wgu.shape[1:], lambda i, s, offs: (s, 0, 0)),
            pl.BlockSpec((1,) + wd.shape[1:], lambda i, s, offs: (s, 0, 0)),
            pl.BlockSpec((1, d), const2),
            pl.BlockSpec((1, d), const2),
        ],
        out_specs=pl.BlockSpec((MOE_TILE, d), lambda i, s, offs: (i, 0)),
        scratch_shapes=[
            pltpu.VMEM((MOE_TILE, MOE_TILE), jnp.bfloat16),
            pltpu.VMEM((MOE_TILE // MOE_BLK, d, MOE_BLK), jnp.bfloat16),
            pltpu.VMEM((MOE_TILE // MOE_BLK, ROUTER_LANES, MOE_BLK), jnp.float32),
            pltpu.VMEM((MOE_TILE // MOE_BLK, d, MOE_BLK), jnp.float32),
        ],
    )
    return pl.pallas_call(
        functools.partial(_moe_kernel, alpha=alpha),
        grid_spec=grid_spec,
        out_shape=jax.ShapeDtypeStruct((n, d), jnp.float32),
        compiler_params=_cparams(("parallel", "arbitrary")), name="moe",
    )(offs, h, comb, wgu, wd, g, beta)


def _moe_weights_kernel(wg_ref, wu_ref, wd_ref, wgu_ref, wdt_ref):
    wgu_ref[0, 0] = wg_ref[0, 0, 0].T.astype(wgu_ref.dtype)
    wgu_ref[0, 1] = wu_ref[0, 0, 0].T.astype(wgu_ref.dtype)
    wdt_ref[0] = wd_ref[0, 0, 0].T.astype(wdt_ref.dtype)


def _moe_weights(w_gate, w_up, w_down, layer):
    _, ng, ne, d, f = w_gate.shape
    w_spec = lambda a, b: pl.BlockSpec((1, 1, 1, a, b), lambda g, e: (layer, g, e, 0, 0))
    wgu, wdt = pl.pallas_call(
        _moe_weights_kernel,
        grid=(ng, ne),
        in_specs=[w_spec(d, f), w_spec(d, f), w_spec(f, d)],
        out_specs=[pl.BlockSpec((1, 2, f, d), lambda g, e: (g * ne + e, 0, 0, 0)),
                   pl.BlockSpec((1, d, f), lambda g, e: (g, 0, e))],
        out_shape=[jax.ShapeDtypeStruct((ng * ne, 2, f, d), jnp.bfloat16),
                   jax.ShapeDtypeStruct((ng, d, ne * f), jnp.bfloat16)],
        compiler_params=_cparams(("parallel", "parallel")), name="moe_weights",
    )(w_gate, w_up, w_down)
    return wgu.reshape(ng, ne * 2 * f, d), wdt


def _group_offsets(cnt, tiles_per_moe_tile):
    c = cnt.reshape(-1, tiles_per_moe_tile, ROUTER_LANES)[:, :, :N_GROUPS].sum(axis=1)
    ends = jnp.cumsum(c, axis=-1)
    offs = jnp.concatenate([jnp.zeros_like(ends[:, :1]), ends], axis=-1)
    return offs.astype(jnp.int32)


def _rope_tables(seq):
    inv_freq = 1.0 / (ROPE_THETA ** (jnp.arange(0, QK_ROPE, 2, dtype=jnp.float32) / QK_ROPE))
    ang = jnp.arange(seq, dtype=jnp.float32)[:, None] * inv_freq[None, :]
    cos, sin = jnp.cos(ang), jnp.sin(ang)
    c96 = jnp.concatenate([jnp.ones((seq, QK_NOPE), jnp.float32), cos, cos], axis=-1)
    s96 = jnp.concatenate([jnp.zeros((seq, QK_NOPE), jnp.float32), -sin, sin], axis=-1)
    return _pad_cols(c96, 0, K_LANES), _pad_cols(s96, 0, K_LANES), c96.T, s96.T


def _swap_halves(w):
    half = w.shape[-1] // 2
    return jnp.concatenate([w[..., half:], w[..., :half]], axis=-1)


def _pad_cols(w, left, total):
    return jnp.pad(w, [(0, 0)] * (w.ndim - 1) + [(left, total - left - w.shape[-1])])


def _pick_tile(n, want):
    t = min(n, want)
    while n % t:
        t //= 2
    return t


def kernel(x, ln_in_g, ln_in_b, w_in, g_q, g_kv, w_uq, w_ukv, w_conv, w_out, ln1_g, ln1_b, w_group, b_group,
           w_router, b_router, w_gate, w_up, w_down, ln2_g, ln2_b):
    bsz, seq, d = x.shape
    depth = w_in.shape[0]
    alpha = (2.0 * depth) ** 0.25
    bf16 = jnp.bfloat16
    tm = _pick_tile(seq, 512)
    tq = _pick_tile(seq, 1024)
    tk = _attn_kv_chunk(seq, tm, tq)

    c96, s96, c96t, s96t = _rope_tables(seq)
    cc, sc, m1a, m1b, m2 = _dft_tables(seq)
    row = lambda v: v.reshape(1, -1)
    d_mla = N_HEADS * V_DIM

    h = x
    for l in range(depth):
        wi = w_in[l]
        splits = np.cumsum([Q_LORA, KV_LORA, QK_ROPE, D_CONV, D_CONV, D_CONV])
        w_cq, w_ckv, w_kpe, w_b, w_c, w_h, w_f = jnp.split(wi, splits, axis=-1)
        win = jnp.concatenate([
            w_cq, w_ckv, w_b, w_c, w_h, w_f,
            _pad_cols(w_kpe, QK_NOPE, K_LANES), _pad_cols(_swap_halves(w_kpe), QK_NOPE, K_LANES)],
            axis=-1).astype(bf16)
        wq = jnp.transpose(w_uq[l], (1, 0, 2))
        wqs = _pad_cols(_swap_halves(wq[..., QK_NOPE:]), QK_NOPE, D_QK)
        wqt = jnp.transpose(wq, (0, 2, 1)).reshape(N_HEADS * D_QK, Q_LORA)
        wqst = jnp.transpose(wqs, (0, 2, 1)).reshape(N_HEADS * D_QK, Q_LORA)
        wkv = jnp.transpose(w_ukv[l], (1, 0, 2))
        wk = _pad_cols(wkv[..., :QK_NOPE], 0, K_LANES)
        wvt = _pad_cols(wkv[..., QK_NOPE:], 0, V_ROWS)
        wvt = jnp.transpose(wvt, (0, 2, 1)).reshape(N_HEADS * V_ROWS, KV_LORA)
        outs = _inproj(h, row(ln_in_g), row(ln_in_b), win, row(g_q[l]), row(g_kv[l]), wqt.astype(bf16),
                       wqst.astype(bf16), wk.astype(bf16), wvt.astype(bf16), c96, s96, c96t, s96t, cc, sc,
                       ln_input=(l == 0), tm=tm, tq=tq, tk=tk)
        if l == 0:
            h, outs = outs[0], outs[1:]
        qt, k, vt, u, bgate, ab = outs
        o_mla = _attention(qt, k, vt)
        o_four = _fourier(ab, m1a, m1b, m2)

        wo = w_out[l].astype(bf16)
        wom = wo[:d_mla]
        woc = wo[d_mla:d_mla + D_CONV]
        wof = wo[d_mla + D_CONV:]
        wr = _pad_cols(jnp.concatenate([w_group[l], w_router[l].reshape(d, N_EXPERTS)], axis=-1), 0, ROUTER_LANES)
        br = _pad_cols(jnp.concatenate([b_group[l], b_router[l].reshape(N_EXPERTS)])[None, :], 0, ROUTER_LANES)
        wrh = wr.astype(bf16)
        wrl = (wr - wrh.astype(jnp.float32)).astype(bf16)
        h1, comb, cnt = _outproj(o_mla, u, bgate, o_four, h, wom, woc, wof, w_conv[l], row(ln1_g[l]), row(ln1_b[l]),
                            wrh, wrl, br, alpha=alpha, tm=tm)

        wgu, wd = _moe_weights(w_gate, w_up, w_down, l)
        offs = _group_offsets(cnt, MOE_TILE // tm)
        h = _moe(h1.reshape(bsz * seq, d), comb.reshape(bsz * seq, ROUTER_LANES), offs, wgu, wd,
                 row(ln2_g[l]), row(ln2_b[l]), alpha=alpha).reshape(bsz, seq, d)
    return h
```
